```python
import jax, jax.numpy as jnp
from jax import lax
import numpy as np

D_MODEL = 1024
BATCH = 8
SEQ = 2048
DEPTH = 1
DEC_BATCH = 128
DEC_SEQ = 8
PAST_LEN = 16384
PAGE_SIZE = 128

CHUNK = 128
A_GROUPS = 8
A_GROUP_DIM = D_MODEL // 8
A_WIDTH = A_GROUPS * A_GROUP_DIM
POOL_WINDOWS = (2, 4, 8, 16)
POOL_GROUPS = len(POOL_WINDOWS)
POOL_GROUP_DIM = D_MODEL // 8
POOL_WIDTH = POOL_GROUPS * POOL_GROUP_DIM
POOL_HIST = max(POOL_WINDOWS) - 1
MEM_LEN = 256
MEM_HEADS = 4
MEM_HEAD_DIM = D_MODEL // 8
MEM_WIDTH = MEM_HEADS * MEM_HEAD_DIM
N_BRANCHES = 3
OFF_P = 2 * A_WIDTH
OFF_Q = OFF_P + POOL_WIDTH
OFF_G = OFF_Q + MEM_WIDTH
C_IN = OFF_G + N_BRANCHES * D_MODEL
N_EXPERTS = 32
TOP_K = 4
D_FF = D_MODEL
SWIGLU_LIMIT = 7.0
SWIGLU_ALPHA = 1.702
EXPERT_BLOCK = 128
LN_EPS = 1e-5
DN_ALPHA = (2 * DEPTH) ** 0.25
DN_BETA = (8 * DEPTH) ** -0.25

kernel_name = 'gated_hybrid_chunkmlp_pool_memattn_moe'


def layer_norm(x, g, b):
    xf = x.astype(jnp.float32)
    mu = jnp.mean(xf, axis=-1, keepdims=True)
    xc = xf - mu
    var = jnp.mean(xc * xc, axis=-1, keepdims=True)
    return (xc * lax.rsqrt(var + LN_EPS) * g + b).astype(x.dtype)


def spatial_gating(v, w_s, b_s):
    L = v.shape[2]
    mask = jnp.tril(jnp.ones((L, L), dtype=bool))
    w = jnp.where(mask, w_s[:, :L, :L], 0.0).astype(v.dtype)
    out = jnp.einsum('gts,ncsge->nctge', w, v)
    return out + b_s[:, :L].T.astype(v.dtype)[None, None, :, :, None]


def pool_mix(p_ext, pos0, w_pool, ls_pool):
    N = p_ext.shape[0]
    L = p_ext.shape[1] - POOL_HIST
    pf = p_ext.astype(jnp.float32)
    cs = jnp.concatenate([jnp.zeros_like(pf[:, :1]), jnp.cumsum(pf, axis=1)], axis=1)
    pos = pos0 + jnp.arange(L)
    tok = pf[:, POOL_HIST:]
    outs = []
    for g, w in enumerate(POOL_WINDOWS):
        c0, c1 = g * POOL_GROUP_DIM, (g + 1) * POOL_GROUP_DIM
        wsum = cs[:, POOL_HIST + 1:, c0:c1] - cs[:, POOL_HIST + 1 - w:POOL_HIST + 1 - w + L, c0:c1]
        cnt = jnp.minimum(w, pos + 1).astype(jnp.float32)[None, :, None]
        outs.append(wsum / cnt - tok[:, :, c0:c1])
    d = jnp.stack(outs, axis=2).astype(p_ext.dtype)
    y = jnp.einsum('nlgc,gcd->nlgd', d, w_pool).reshape(N, L, POOL_WIDTH)
    return y * ls_pool


def mem_attend(q, k, v):
    s = jnp.einsum('nlhd,nmhd->nhlm', q, k).astype(jnp.float32) * (MEM_HEAD_DIM ** -0.5)
    pr = jax.nn.softmax(s, axis=-1).astype(v.dtype)
    return jnp.einsum('nhlm,nmhd->nlhd', pr, v)


def token_mix(x, pool_prev, mem_k, mem_v, pos0, w_in, b_in, ln_v_g, ln_v_b, w_s, b_s,
              w_a_out, w_pool, ls_pool, w_b_out, w_m_out, w_o):
    N, L, _ = x.shape
    proj = x @ w_in + b_in
    z = jax.nn.gelu(proj[..., :OFF_P])
    u, v = z[..., :A_WIDTH], z[..., A_WIDTH:]
    p = proj[..., OFF_P:OFF_Q]
    q = proj[..., OFF_Q:OFF_G]
    gates = jax.nn.sigmoid(proj[..., OFF_G:].reshape(N, L, N_BRANCHES, D_MODEL))
    v = layer_norm(v, ln_v_g, ln_v_b)
    cl = min(L, CHUNK)
    mixed = spatial_gating(v.reshape(N, L // cl, cl, A_GROUPS, A_GROUP_DIM), w_s, b_s)
    br_a = (u * mixed.reshape(N, L, A_WIDTH)) @ w_a_out
    p_ext = jnp.concatenate([pool_prev.astype(p.dtype), p], axis=1)
    br_b = pool_mix(p_ext, pos0, w_pool, ls_pool) @ w_b_out
    o = mem_attend(q.reshape(N, L, MEM_HEADS, MEM_HEAD_DIM), mem_k.astype(q.dtype), mem_v.astype(q.dtype))
    br_m = o.reshape(N, L, MEM_WIDTH) @ w_m_out
    h = gates[:, :, 0] * br_a + gates[:, :, 1] * br_b + gates[:, :, 2] * br_m
    return h @ w_o, p_ext[:, -POOL_HIST:], v


def moe_ffn(x2d, w_r, b_r, w1, b1, w2, b2):
    T, D = x2d.shape
    logits = (x2d @ w_r + b_r).astype(jnp.float32)
    top_v, top_e = lax.top_k(logits, TOP_K)
    gate = jax.nn.softmax(top_v, axis=-1).astype(x2d.dtype)
    n_assign = T * TOP_K
    e_flat = top_e.reshape(n_assign)
    tok_flat = jnp.arange(n_assign, dtype=jnp.int32) // TOP_K
    g_flat = gate.reshape(n_assign)
    order = jnp.argsort(e_flat)
    e_sorted = e_flat[order]
    counts = jnp.bincount(e_flat, length=N_EXPERTS)
    padded = (counts + EXPERT_BLOCK - 1) // EXPERT_BLOCK * EXPERT_BLOCK
    pad_end = jnp.cumsum(padded)
    pad_start = pad_end - padded
    start = jnp.cumsum(counts) - counts
    dest = pad_start[e_sorted] + jnp.arange(n_assign, dtype=pad_end.dtype) - start[e_sorted]
    n_blocks = -(-n_assign // EXPERT_BLOCK) + N_EXPERTS
    n_slots = n_blocks * EXPERT_BLOCK
    slot_tok = jnp.full((n_slots,), T, jnp.int32).at[dest].set(tok_flat[order])
    slot_g = jnp.zeros((n_slots,), x2d.dtype).at[dest].set(g_flat[order])
    blk_start = jnp.arange(n_blocks, dtype=pad_end.dtype) * EXPERT_BLOCK
    blk_e = jnp.minimum(jnp.searchsorted(pad_end, blk_start, side='right'), N_EXPERTS - 1)
    x_pad = jnp.concatenate([x2d, jnp.zeros((1, D), x2d.dtype)], axis=0)
    xs = x_pad[slot_tok].reshape(n_blocks, EXPERT_BLOCK, D)

    def expert_block(args):
        xb, e = args
        h = xb @ w1[e] + b1[e]
        g = jnp.minimum(h[:, :D_FF], SWIGLU_LIMIT)
        lin = jnp.clip(h[:, D_FF:], -SWIGLU_LIMIT, SWIGLU_LIMIT)
        act = g * jax.nn.sigmoid(SWIGLU_ALPHA * g) * (lin + 1.0)
        return act @ w2[e] + b2[e]

    out = lax.map(expert_block, (xs, blk_e)).reshape(n_slots, D)
    y = jnp.zeros((T + 1, D), x2d.dtype).at[slot_tok].add(out * slot_g[:, None])
    return y[:T]


def setup_inputs(seed: int = 0) -> dict:
    key = jax.random.key(seed)
    ks = jax.random.split(key, 32)

    def nrm(k, shape, scale):
        return jax.random.normal(k, shape, jnp.float32) * scale

    tri = jnp.tril(jnp.ones((CHUNK, CHUNK), jnp.float32))
    return {
        'x_prompt': nrm(ks[0], (BATCH, SEQ, D_MODEL), 1.0),
        'x_sample': nrm(ks[1], (DEC_BATCH, DEC_SEQ, D_MODEL), 1.0),
        'state_pool': nrm(ks[2], (DEPTH, DEC_BATCH, POOL_HIST, POOL_WIDTH), 1.0),
        'cache_mem_k': nrm(ks[3], (DEPTH, DEC_BATCH, MEM_LEN, MEM_HEADS, MEM_HEAD_DIM), 1.0),
        'cache_mem_v': nrm(ks[4], (DEPTH, DEC_BATCH, MEM_LEN, MEM_HEADS, MEM_HEAD_DIM), 1.0),
        'mem_prompt': nrm(ks[5], (BATCH, MEM_LEN, D_MODEL), 1.0),
        'w_in': nrm(ks[6], (DEPTH, D_MODEL, C_IN), D_MODEL ** -0.5),
        'b_in': nrm(ks[7], (DEPTH, C_IN), 0.02),
        'ln_v_g': 1.0 + nrm(ks[8], (DEPTH, A_WIDTH), 0.05),
        'ln_v_b': nrm(ks[9], (DEPTH, A_WIDTH), 0.02),
        'w_s': nrm(ks[10], (DEPTH, A_GROUPS, CHUNK, CHUNK), CHUNK ** -0.5) * tri,
        'b_s': 1.0 + nrm(ks[11], (DEPTH, A_GROUPS, CHUNK), 0.1),
        'w_a_out': nrm(ks[12], (DEPTH, A_WIDTH, D_MODEL), A_WIDTH ** -0.5),
        'w_pool': nrm(ks[13], (DEPTH, POOL_GROUPS, POOL_GROUP_DIM, POOL_GROUP_DIM), POOL_GROUP_DIM ** -0.5),
        'ls_pool': 1.0 + nrm(ks[14], (DEPTH, POOL_WIDTH), 0.1),
        'w_b_out': nrm(ks[15], (DEPTH, POOL_WIDTH, D_MODEL), POOL_WIDTH ** -0.5),
        'w_mk': nrm(ks[16], (DEPTH, D_MODEL, MEM_WIDTH), D_MODEL ** -0.5),
        'w_mv': nrm(ks[17], (DEPTH, D_MODEL, MEM_WIDTH), D_MODEL ** -0.5),
        'w_m_out': nrm(ks[18], (DEPTH, MEM_WIDTH, D_MODEL), MEM_WIDTH ** -0.5),
        'w_o': nrm(ks[19], (DEPTH, D_MODEL, D_MODEL), D_MODEL ** -0.5 * DN_BETA),
        'ln1_g': 1.0 + nrm(ks[20], (DEPTH, D_MODEL), 0.05),
        'ln1_b': nrm(ks[21], (DEPTH, D_MODEL), 0.02),
        'w_r': nrm(ks[22], (DEPTH, D_MODEL, N_EXPERTS), D_MODEL ** -0.5),
        'b_r': nrm(ks[23], (DEPTH, N_EXPERTS), 0.01),
        'w1': nrm(ks[24], (DEPTH, N_EXPERTS, D_MODEL, 2 * D_FF), D_MODEL ** -0.5),
        'b1': nrm(ks[25], (DEPTH, N_EXPERTS, 2 * D_FF), 0.02),
        'w2': nrm(ks[26], (DEPTH, N_EXPERTS, D_FF, D_MODEL), D_FF ** -0.5 * DN_BETA),
        'b2': nrm(ks[27], (DEPTH, N_EXPERTS, D_MODEL), 0.02),
        'ln2_g': 1.0 + nrm(ks[28], (DEPTH, D_MODEL), 0.05),
        'ln2_b': nrm(ks[29], (DEPTH, D_MODEL), 0.02),
    }


def reference(x_prompt, x_sample, state_pool, cache_mem_k, cache_mem_v, mem_prompt,
              w_in, b_in, ln_v_g, ln_v_b, w_s, b_s, w_a_out, w_pool, ls_pool, w_b_out,
              w_mk, w_mv, w_m_out, w_o, ln1_g, ln1_b, w_r, b_r, w1, b1, w2, b2, ln2_g, ln2_b):
    y_p, y_s = x_prompt, x_sample
    pool_p_out, pool_s_out, mk_out, mv_out, cv_out = [], [], [], [], []
    for l in range(DEPTH):
        mk_p = (mem_prompt @ w_mk[l]).reshape(BATCH, MEM_LEN, MEM_HEADS, MEM_HEAD_DIM)
        mv_p = (mem_prompt @ w_mv[l]).reshape(BATCH, MEM_LEN, MEM_HEADS, MEM_HEAD_DIM)
        shared = (w_in[l], b_in[l], ln_v_g[l], ln_v_b[l], w_s[l], b_s[l], w_a_out[l],
                  w_pool[l], ls_pool[l], w_b_out[l], w_m_out[l], w_o[l])
        zero_hist = jnp.zeros((y_p.shape[0], POOL_HIST, POOL_WIDTH), y_p.dtype)
        t_p, pool_p, _ = token_mix(y_p, zero_hist, mk_p, mv_p, 0, *shared)
        t_s, pool_s, v_s = token_mix(y_s, state_pool[l], cache_mem_k[l], cache_mem_v[l], PAST_LEN, *shared)
        y_p = layer_norm(DN_ALPHA * y_p + t_p, ln1_g[l], ln1_b[l])
        y_s = layer_norm(DN_ALPHA * y_s + t_s, ln1_g[l], ln1_b[l])
        f_p = moe_ffn(y_p.reshape(-1, D_MODEL), w_r[l], b_r[l], w1[l], b1[l], w2[l], b2[l]).reshape(y_p.shape)
        f_s = moe_ffn(y_s.reshape(-1, D_MODEL), w_r[l], b_r[l], w1[l], b1[l], w2[l], b2[l]).reshape(y_s.shape)
        y_p = layer_norm(DN_ALPHA * y_p + f_p, ln2_g[l], ln2_b[l])
        y_s = layer_norm(DN_ALPHA * y_s + f_s, ln2_g[l], ln2_b[l])
        pool_p_out.append(pool_p)
        pool_s_out.append(pool_s)
        mk_out.append(mk_p)
        mv_out.append(mv_p)
        cv_out.append(v_s)
    return (y_p, y_s, jnp.stack(pool_p_out), jnp.stack(pool_s_out), jnp.stack(mk_out), jnp.stack(mv_out), jnp.stack(cv_out))
```

```python
import functools

import jax
import jax.numpy as jnp
from jax import lax
from jax.experimental import pallas as pl
from jax.experimental.pallas import tpu as pltpu

F32 = jnp.float32
BF16 = jnp.bfloat16

D_MODEL = 1024
CHUNK = 128
A_GROUPS = 8
A_GROUP_DIM = 128
A_WIDTH = A_GROUPS * A_GROUP_DIM
POOL_WINDOWS = (2, 4, 8, 16)
POOL_GROUP_DIM = 128
POOL_WIDTH = len(POOL_WINDOWS) * POOL_GROUP_DIM
POOL_HIST = max(POOL_WINDOWS) - 1
MEM_LEN = 256
MEM_HEADS = 4
MEM_HEAD_DIM = 128
MEM_WIDTH = MEM_HEADS * MEM_HEAD_DIM
OFF_P = 2 * A_WIDTH
OFF_Q = OFF_P + POOL_WIDTH
OFF_G = OFF_Q + MEM_WIDTH
C_IN = OFF_G + 3 * D_MODEL
N_EXPERTS = 32
TOP_K = 4
D_FF = D_MODEL
SWIGLU_LIMIT = 7.0
SWIGLU_ALPHA = 1.702
LN_EPS = 1e-5

V7X_VMEM_LIMIT_BYTES = 56 * 1024 * 1024

PROMPT_ROWS = 256
SAMPLE_SEQS = 8
ROUTER_ROWS = 512
MOVE_ROWS = 256
EXPERT_ROWS = 256
HIST_ROWS = 16


def _const_spec(shape):
    nd = len(shape)
    return pl.BlockSpec(shape, lambda *_: (0,) * nd, pipeline_mode=pl.Buffered(1))


def _layer_norm(x, g, b):
    mu = jnp.mean(x, axis=-1, keepdims=True)
    xc = x - mu
    var = jnp.mean(xc * xc, axis=-1, keepdims=True)
    return xc * lax.rsqrt(var + LN_EPS) * g + b


def _dot(a, b):
    return jnp.dot(a, b, preferred_element_type=F32)


def _softmax_rows(s):
    m = jnp.max(s, axis=-1, keepdims=True)
    e = jnp.exp(s - m)
    return e / jnp.sum(e, axis=-1, keepdims=True)


def _memproj_kernel(x_ref, w_ref, k_ref, v_ref):
    y = _dot(x_ref[...].astype(BF16), w_ref[...])
    k_ref[...] = y[:, :MEM_WIDTH]
    v_ref[...] = y[:, MEM_WIDTH:]


def _memproj(mem2d, w_kv):
    rows = mem2d.shape[0]
    blk = 512 if rows % 512 == 0 else MEM_LEN
    return pl.pallas_call(
        _memproj_kernel,
        grid=(rows // blk,),
        in_specs=[pl.BlockSpec((blk, D_MODEL), lambda i: (i, 0)),
                  _const_spec((D_MODEL, 2 * MEM_WIDTH))],
        out_specs=[pl.BlockSpec((blk, MEM_WIDTH), lambda i: (i, 0)),
                   pl.BlockSpec((blk, MEM_WIDTH), lambda i: (i, 0))],
        out_shape=[jax.ShapeDtypeStruct((rows, MEM_WIDTH), F32)] * 2,
        name="memproj",
    )(mem2d, w_kv)


def _project(xb, w_in_ref, b_in_ref, lo, hi):
    return _dot(xb, w_in_ref[:, lo:hi]) + b_in_ref[:, lo:hi]


def _merge_and_norm(x, xb, br_a, br_b, br_m, w_in_ref, b_in_ref, wo_ref, g_ref, b_ref, alpha):
    h = jax.nn.sigmoid(_project(xb, w_in_ref, b_in_ref, OFF_G, OFF_G + D_MODEL)) * br_a
    h += jax.nn.sigmoid(_project(xb, w_in_ref, b_in_ref, OFF_G + D_MODEL, OFF_G + 2 * D_MODEL)) * br_b
    h += jax.nn.sigmoid(_project(xb, w_in_ref, b_in_ref, OFF_G + 2 * D_MODEL, C_IN)) * br_m
    t = _dot(h.astype(BF16), wo_ref[...])
    return _layer_norm(alpha * x + t, g_ref[...], b_ref[...])


def _pool_branch(window_sum, tok, cnt, wpool_ref, ls_ref, wb_ref, rows):
    ys = []
    for g in range(len(POOL_WINDOWS)):
        d = window_sum(g) / cnt(g) - tok(g)
        ys.append(_dot(d.reshape(rows, POOL_GROUP_DIM).astype(BF16), wpool_ref[g]))
    y = jnp.concatenate(ys, axis=1) * ls_ref[...]
    return _dot(y.astype(BF16), wb_ref[...])


def _mix_prompt_kernel(x_ref, mk_ref, mv_ref, w_in_ref, b_in_ref, lnv_g_ref, lnv_b_ref, ws_ref,
                       bs_ref, wa_ref, wpool_ref, ls_ref, wb_ref, wm_ref, wo_ref, ln1_g_ref,
                       ln1_b_ref, y_ref, pool_ref, pbuf, *, alpha):
    j = pl.program_id(1)
    rows = x_ref.shape[0]
    x = x_ref[...]
    xb = x.astype(BF16)

    z = jax.nn.gelu(_project(xb, w_in_ref, b_in_ref, 0, OFF_P))
    u = z[:, :A_WIDTH]
    vn = _layer_norm(z[:, A_WIDTH:], lnv_g_ref[...], lnv_b_ref[...]).astype(BF16)
    tril = (lax.broadcasted_iota(jnp.int32, (CHUNK, CHUNK), 1)
            <= lax.broadcasted_iota(jnp.int32, (CHUNK, CHUNK), 0))
    w_tril = [jnp.where(tril, ws_ref[g], 0.0).astype(BF16) for g in range(A_GROUPS)]
    mixed = []
    for c in range(rows // CHUNK):
        vc = vn[c * CHUNK:(c + 1) * CHUNK]
        cols = [_dot(w_tril[g], vc[:, g * A_GROUP_DIM:(g + 1) * A_GROUP_DIM])
                for g in range(A_GROUPS)]
        mixed.append(jnp.concatenate(cols, axis=1) + bs_ref[...])
    mixed = jnp.concatenate(mixed, axis=0)
    br_a = _dot((u * mixed).astype(BF16), wa_ref[...])

    @pl.when(j == 0)
    def _():
        pbuf[0:HIST_ROWS, :] = jnp.zeros((HIST_ROWS, POOL_WIDTH), F32)

    pbuf[HIST_ROWS:HIST_ROWS + rows, :] = _project(xb, w_in_ref, b_in_ref, OFF_P, OFF_Q)
    pos = j * rows + lax.broadcasted_iota(jnp.int32, (rows, POOL_GROUP_DIM), 0)

    def window_sum(g):
        lo = g * POOL_GROUP_DIM
        acc = pbuf[HIST_ROWS:HIST_ROWS + rows, lo:lo + POOL_GROUP_DIM]
        for i in range(1, POOL_WINDOWS[g]):
            acc = acc + pbuf[HIST_ROWS - i:HIST_ROWS - i + rows, lo:lo + POOL_GROUP_DIM]
        return acc

    def tok(g):
        lo = g * POOL_GROUP_DIM
        return pbuf[HIST_ROWS:HIST_ROWS + rows, lo:lo + POOL_GROUP_DIM]

    def cnt(g):
        return jnp.minimum(POOL_WINDOWS[g], pos + 1).astype(F32)

    br_b = _pool_branch(window_sum, tok, cnt, wpool_ref, ls_ref, wb_ref, rows)
    pool_ref[...] = pbuf[rows + 1:rows + HIST_ROWS, :]
    pbuf[0:HIST_ROWS, :] = pbuf[rows:rows + HIST_ROWS, :]

    q = _project(xb, w_in_ref, b_in_ref, OFF_Q, OFF_G).astype(BF16)
    mk = mk_ref[...].astype(BF16)
    mv = mv_ref[...].astype(BF16)
    outs = []
    for h in range(MEM_HEADS):
        sl = slice(h * MEM_HEAD_DIM, (h + 1) * MEM_HEAD_DIM)
        s = lax.dot_general(q[:, sl], mk[:, sl], (((1,), (1,)), ((), ())),
                            preferred_element_type=F32) * (MEM_HEAD_DIM ** -0.5)
        outs.append(_dot(_softmax_rows(s).astype(BF16), mv[:, sl]))
    br_m = _dot(jnp.concatenate(outs, axis=1).astype(BF16), wm_ref[...])

    y_ref[...] = _merge_and_norm(x, xb, br_a, br_b, br_m, w_in_ref, b_in_ref, wo_ref,
                                 ln1_g_ref, ln1_b_ref, alpha)


def _mix_prompt(x2d, mk, mv, wts, *, batch, seq, alpha):
    rows = PROMPT_ROWS
    steps = seq // rows
    weight_specs = [_const_spec(w.shape) for w in wts]
    return pl.pallas_call(
        functools.partial(_mix_prompt_kernel, alpha=alpha),
        grid=(batch, steps),
        in_specs=[pl.BlockSpec((rows, D_MODEL), lambda n, j: (n * steps + j, 0)),
                  pl.BlockSpec((MEM_LEN, MEM_WIDTH), lambda n, j: (n, 0)),
                  pl.BlockSpec((MEM_LEN, MEM_WIDTH), lambda n, j: (n, 0))] + weight_specs,
        out_specs=[pl.BlockSpec((rows, D_MODEL), lambda n, j: (n * steps + j, 0)),
                   pl.BlockSpec((None, POOL_HIST, POOL_WIDTH), lambda n, j: (n, 0, 0))],
        out_shape=[jax.ShapeDtypeStruct((batch * seq, D_MODEL), F32),
                   jax.ShapeDtypeStruct((batch, POOL_HIST, POOL_WIDTH), F32)],
        scratch_shapes=[pltpu.VMEM((HIST_ROWS + rows, POOL_WIDTH), F32)],
        compiler_params=pltpu.CompilerParams(
            dimension_semantics=("arbitrary", "arbitrary"),
            vmem_limit_bytes=V7X_VMEM_LIMIT_BYTES),
        name="mix_prompt",
    )(x2d, mk, mv, *wts)


def _mix_sample_kernel(x_ref, hist_ref, kc_ref, vc_ref, w_in_ref, b_in_ref, lnv_g_ref, lnv_b_ref,
                       ws_ref, bs_ref, wa_ref, wpool_ref, ls_ref, wb_ref, wm_ref, wo_ref,
                       ln1_g_ref, ln1_b_ref, y_ref, pool_ref, vn_ref, pbuf, *, alpha, pos0, dec):
    rows = x_ref.shape[0]
    seqs = rows // dec
    x = x_ref[...]
    xb = x.astype(BF16)

    z = jax.nn.gelu(_project(xb, w_in_ref, b_in_ref, 0, OFF_P))
    u = z[:, :A_WIDTH]
    vn = _layer_norm(z[:, A_WIDTH:], lnv_g_ref[...], lnv_b_ref[...])
    vn_ref[...] = vn
    vnb = vn.astype(BF16)
    first = dec * lax.broadcasted_iota(jnp.int32, (seqs, dec, rows), 0)
    t = lax.broadcasted_iota(jnp.int32, (seqs, dec, rows), 1)
    c = lax.broadcasted_iota(jnp.int32, (seqs, dec, rows), 2)
    keep = (c >= first) & (c <= first + t)
    cols = []
    for g in range(A_GROUPS):
        wg = jnp.where(keep, ws_ref[g].reshape(seqs, dec, rows), 0.0)
        wg = wg.reshape(rows, rows).astype(BF16)
        cols.append(_dot(wg, vnb[:, g * A_GROUP_DIM:(g + 1) * A_GROUP_DIM]))
    mixed = jnp.concatenate(cols, axis=1) + bs_ref[...]
    br_a = _dot((u * mixed).astype(BF16), wa_ref[...])

    pbuf[:, 1:HIST_ROWS, :] = hist_ref[...]
    pbuf[:, HIST_ROWS:HIST_ROWS + dec, :] = _project(
        xb, w_in_ref, b_in_ref, OFF_P, OFF_Q).reshape(seqs, dec, POOL_WIDTH)
    pos = pos0 + lax.broadcasted_iota(jnp.int32, (seqs, dec, POOL_GROUP_DIM), 1)

    def window_sum(g):
        lo = g * POOL_GROUP_DIM
        acc = pbuf[:, HIST_ROWS:HIST_ROWS + dec, lo:lo + POOL_GROUP_DIM]
        for i in range(1, POOL_WINDOWS[g]):
            acc = acc + pbuf[:, HIST_ROWS - i:HIST_ROWS - i + dec, lo:lo + POOL_GROUP_DIM]
        return acc

    def tok(g):
        lo = g * POOL_GROUP_DIM
        return pbuf[:, HIST_ROWS:HIST_ROWS + dec, lo:lo + POOL_GROUP_DIM]

    def cnt(g):
        return jnp.minimum(POOL_WINDOWS[g], pos + 1).astype(F32)

    br_b = _pool_branch(window_sum, tok, cnt, wpool_ref, ls_ref, wb_ref, rows)
    pool_ref[...] = pbuf[:, dec + 1:dec + HIST_ROWS, :]

    q = _project(xb, w_in_ref, b_in_ref, OFF_Q, OFF_G).astype(BF16)
    outs = []
    for h in range(MEM_HEADS):
        sl = slice(h * MEM_HEAD_DIM, (h + 1) * MEM_HEAD_DIM)
        qh = q[:, sl].reshape(seqs, dec, MEM_HEAD_DIM)
        kh = kc_ref[:, :, sl].astype(BF16)
        vh = vc_ref[:, :, sl].astype(BF16)
        s = jnp.einsum("sld,smd->slm", qh, kh,
                       preferred_element_type=F32) * (MEM_HEAD_DIM ** -0.5)
        o = jnp.einsum("slm,smd->sld", _softmax_rows(s).astype(BF16), vh,
                       preferred_element_type=F32)
        outs.append(o.reshape(rows, MEM_HEAD_DIM))
    br_m = _dot(jnp.concatenate(outs, axis=1).astype(BF16), wm_ref[...])

    y_ref[...] = _merge_and_norm(x, xb, br_a, br_b, br_m, w_in_ref, b_in_ref, wo_ref,
                                 ln1_g_ref, ln1_b_ref, alpha)


def _mix_sample(x2d, hist, kc, vc, wts, *, nseq, dec, pos0, alpha):
    seqs = SAMPLE_SEQS
    rows = seqs * dec
    weight_specs = [_const_spec(w.shape) for w in wts]
    return pl.pallas_call(
        functools.partial(_mix_sample_kernel, alpha=alpha, pos0=pos0, dec=dec),
        grid=(nseq // seqs,),
        in_specs=[pl.BlockSpec((rows, D_MODEL), lambda i: (i, 0)),
                  pl.BlockSpec((seqs, POOL_HIST, POOL_WIDTH), lambda i: (i, 0, 0)),
                  pl.BlockSpec((seqs, MEM_LEN, MEM_WIDTH), lambda i: (i, 0, 0)),
                  pl.BlockSpec((seqs, MEM_LEN, MEM_WIDTH), lambda i: (i, 0, 0))] + weight_specs,
        out_specs=[pl.BlockSpec((rows, D_MODEL), lambda i: (i, 0)),
                   pl.BlockSpec((seqs, POOL_HIST, POOL_WIDTH), lambda i: (i, 0, 0)),
                   pl.BlockSpec((rows, A_WIDTH), lambda i: (i, 0))],
        out_shape=[jax.ShapeDtypeStruct((nseq * dec, D_MODEL), F32),
                   jax.ShapeDtypeStruct((nseq, POOL_HIST, POOL_WIDTH), F32),
                   jax.ShapeDtypeStruct((nseq * dec, A_WIDTH), F32)],
        scratch_shapes=[pltpu.VMEM((seqs, HIST_ROWS + dec, POOL_WIDTH), F32)],
        compiler_params=pltpu.CompilerParams(
            dimension_semantics=("arbitrary",),
            vmem_limit_bytes=V7X_VMEM_LIMIT_BYTES),
        name="mix_sample",
    )(x2d, hist, kc, vc, *wts)


def _split_bf16(a):
    hi = a.astype(BF16)
    lo = (a - hi.astype(F32)).astype(BF16)
    return hi, lo


def _router_kernel(yp_ref, ys_ref, wr_ref, br_ref, e_ref, g_ref, rank_ref, cnt_ref, carry,
                   *, prompt_tiles):
    i = pl.program_id(0)
    rows = yp_ref.shape[0]

    @pl.when(i == 0)
    def _():
        carry[...] = jnp.zeros(carry.shape, F32)

    x = jnp.where(i < prompt_tiles, yp_ref[...], ys_ref[...])
    x_hi, x_lo = _split_bf16(x)
    w_hi, w_lo = _split_bf16(wr_ref[...])
    nt = (((1,), (1,)), ((), ()))
    logits = (lax.dot_general(w_hi, x_hi, nt, preferred_element_type=F32)
              + lax.dot_general(w_hi, x_lo, nt, preferred_element_type=F32)
              + lax.dot_general(w_lo, x_hi, nt, preferred_element_type=F32)) + br_ref[...]

    eid = lax.broadcasted_iota(jnp.int32, (N_EXPERTS, rows), 0)
    work = logits
    vals, idxs, sels = [], [], []
    for _ in range(TOP_K):
        m = jnp.max(work, axis=0, keepdims=True)
        idx = jnp.min(jnp.where(work == m, eid, N_EXPERTS), axis=0, keepdims=True)
        sel = eid == idx
        work = jnp.where(sel, -jnp.inf, work)
        vals.append(m)
        idxs.append(idx)
        sels.append(sel)
    top_v = jnp.concatenate(vals, axis=0)
    ex = jnp.exp(top_v - top_v[0:1])
    g_ref[...] = ex / jnp.sum(ex, axis=0, keepdims=True)
    e_ref[...] = jnp.concatenate(idxs, axis=0)

    onehot = jnp.zeros((N_EXPERTS, rows), F32)
    for sel in sels:
        onehot = onehot + jnp.where(sel, 1.0, 0.0)
    before = (lax.broadcasted_iota(jnp.int32, (rows, rows), 0)
              < lax.broadcasted_iota(jnp.int32, (rows, rows), 1))
    prefix = _dot(onehot.astype(BF16), jnp.where(before, 1.0, 0.0).astype(BF16)) + carry[...]
    ranks = [jnp.sum(jnp.where(sel, prefix, 0.0), axis=0, keepdims=True) for sel in sels]
    rank_ref[...] = jnp.concatenate(ranks, axis=0).astype(jnp.int32)
    total = carry[...] + jnp.sum(onehot, axis=1, keepdims=True)
    carry[...] = total
    cnt_ref[...] = jnp.broadcast_to(total, cnt_ref.shape).astype(jnp.int32)


def _router(y_p, y_s, w_r_t, b_r_col):
    rows = ROUTER_ROWS
    tp, ts = y_p.shape[0], y_s.shape[0]
    pt, st = tp // rows, ts // rows
    total = tp + ts
    tok_spec = pl.BlockSpec((TOP_K, rows), lambda i: (0, i))
    return pl.pallas_call(
        functools.partial(_router_kernel, prompt_tiles=pt),
        grid=(pt + st,),
        in_specs=[pl.BlockSpec((rows, D_MODEL), lambda i: (jnp.minimum(i, pt - 1), 0)),
                  pl.BlockSpec((rows, D_MODEL), lambda i: (jnp.maximum(i - pt, 0), 0)),
                  _const_spec((N_EXPERTS, D_MODEL)),
                  _const_spec((N_EXPERTS, 1))],
        out_specs=[tok_spec, tok_spec, tok_spec,
                   pl.BlockSpec((N_EXPERTS, 128), lambda i: (0, 0))],
        out_shape=[jax.ShapeDtypeStruct((TOP_K, total), jnp.int32),
                   jax.ShapeDtypeStruct((TOP_K, total), F32),
                   jax.ShapeDtypeStruct((TOP_K, total), jnp.int32),
                   jax.ShapeDtypeStruct((N_EXPERTS, 128), jnp.int32)],
        scratch_shapes=[pltpu.VMEM((N_EXPERTS, 1), F32)],
        compiler_params=pltpu.CompilerParams(dimension_semantics=("arbitrary",)),
        name="router",
    )(y_p, y_s, w_r_t, b_r_col)


def _dispatch_kernel(dest_ref, yp_ref, ys_ref, zero_ref, xs_ref, src, sem, *, prompt_tiles, total):
    del zero_ref
    i = pl.program_id(0)
    rows = src.shape[0]
    src[...] = jnp.where(i < prompt_tiles, yp_ref[...], ys_ref[...])
    base = i * rows

    def row_copy(r, d):
        return pltpu.make_async_copy(src.at[pl.ds(r, 1)], xs_ref.at[pl.ds(d, 1)], sem)

    def issue(r, carry):
        for k in range(TOP_K):
            row_copy(r, dest_ref[k * total + base + r]).start()
        return carry

    def drain(r, carry):
        for k in range(TOP_K):
            row_copy(r, dest_ref[k * total + base + r]).wait()
        return carry

    lax.fori_loop(0, rows, issue, 0)
    lax.fori_loop(0, rows, drain, 0)


def _dispatch(dest_flat, y_p, y_s, n_slots):
    rows = MOVE_ROWS
    tp, ts = y_p.shape[0], y_s.shape[0]
    pt, st = tp // rows, ts // rows
    zeros = jnp.zeros((n_slots, D_MODEL), F32)
    return pl.pallas_call(
        functools.partial(_dispatch_kernel, prompt_tiles=pt, total=tp + ts),
        grid_spec=pltpu.PrefetchScalarGridSpec(
            num_scalar_prefetch=1,
            grid=(pt + st,),
            in_specs=[pl.BlockSpec((rows, D_MODEL), lambda i, d: (jnp.minimum(i, pt - 1), 0)),
                      pl.BlockSpec((rows, D_MODEL), lambda i, d: (jnp.maximum(i - pt, 0), 0)),
                      pl.BlockSpec(memory_space=pl.ANY)],
            out_specs=pl.BlockSpec(memory_space=pl.ANY),
            scratch_shapes=[pltpu.VMEM((rows, D_MODEL), F32), pltpu.SemaphoreType.DMA]),
        out_shape=jax.ShapeDtypeStruct((n_slots, D_MODEL), F32),
        input_output_aliases={3: 0},
        compiler_params=pltpu.CompilerParams(dimension_semantics=("arbitrary",)),
        name="dispatch",
    )(dest_flat, y_p, y_s, zeros)


def _experts_kernel(blk_e_ref, nact_ref, x_ref, w1_ref, b1_ref, w2_ref, b2_ref, o_ref):
    del blk_e_ref
    active = pl.program_id(0) < nact_ref[0]

    @pl.when(jnp.logical_not(active))
    def _():
        o_ref[...] = jnp.zeros(o_ref.shape, F32)

    @pl.when(active)
    def _():
        h = _dot(x_ref[...].astype(BF16), w1_ref[...]) + b1_ref[...]
        g = jnp.minimum(h[:, :D_FF], SWIGLU_LIMIT)
        lin = jnp.clip(h[:, D_FF:], -SWIGLU_LIMIT, SWIGLU_LIMIT)
        act = g * jax.nn.sigmoid(SWIGLU_ALPHA * g) * (lin + 1.0)
        o_ref[...] = _dot(act.astype(BF16), w2_ref[...]) + b2_ref[...]


def _experts(blk_e, nact, xs, w1, b1, w2, b2):
    rows = EXPERT_ROWS
    n_slots = xs.shape[0]

    def slot_map(b, blk_e, nact):
        return (jnp.minimum(b, nact[0] - 1), 0)

    def expert_map(b, blk_e, nact):
        return (blk_e[b], 0, 0)

    return pl.pallas_call(
        _experts_kernel,
        grid_spec=pltpu.PrefetchScalarGridSpec(
            num_scalar_prefetch=2,
            grid=(n_slots // rows,),
            in_specs=[pl.BlockSpec((rows, D_MODEL), slot_map),
                      pl.BlockSpec((None, D_MODEL, 2 * D_FF), expert_map),
                      pl.BlockSpec((None, 1, 2 * D_FF), expert_map),
                      pl.BlockSpec((None, D_FF, D_MODEL), expert_map),
                      pl.BlockSpec((None, 1, D_MODEL), expert_map)],
            out_specs=pl.BlockSpec((rows, D_MODEL), lambda b, blk_e, nact: (b, 0))),
        out_shape=jax.ShapeDtypeStruct((n_slots, D_MODEL), F32),
        compiler_params=pltpu.CompilerParams(
            dimension_semantics=("arbitrary",),
            vmem_limit_bytes=V7X_VMEM_LIMIT_BYTES),
        name="experts",
    )(blk_e, nact, xs, w1, b1, w2, b2)


def _combine_kernel(dest_ref, yp_ref, ys_ref, g_ref, os_ref, ln_g_ref, ln_b_ref, op_ref, osm_ref,
                    buf, sem, *, prompt_tiles, total, alpha):
    i = pl.program_id(0)
    rows = yp_ref.shape[0]
    base = i * rows

    def row_copy(k, r, d):
        return pltpu.make_async_copy(os_ref.at[pl.ds(d, 1)], buf.at[k, pl.ds(r, 1)], sem)

    def issue(r, carry):
        for k in range(TOP_K):
            row_copy(k, r, dest_ref[k * total + base + r]).start()
        return carry

    def drain(r, carry):
        for k in range(TOP_K):
            row_copy(k, r, dest_ref[k * total + base + r]).wait()
        return carry

    lax.fori_loop(0, rows, issue, 0)
    lax.fori_loop(0, rows, drain, 0)

    y = jnp.where(i < prompt_tiles, yp_ref[...], ys_ref[...])
    g = g_ref[...]
    f = g[:, 0:1] * buf[0]
    for k in range(1, TOP_K):
        f = f + g[:, k:k + 1] * buf[k]
    out = _layer_norm(alpha * y + f, ln_g_ref[...], ln_b_ref[...])

    @pl.when(i < prompt_tiles)
    def _():
        op_ref[...] = out

    @pl.when(i >= prompt_tiles)
    def _():
        osm_ref[...] = out


def _combine(dest_flat, y_p, y_s, gates, out_sorted, ln_g, ln_b, *, alpha):
    rows = MOVE_ROWS
    tp, ts = y_p.shape[0], y_s.shape[0]
    pt, st = tp // rows, ts // rows

    def p_map(i, d):
        return (jnp.minimum(i, pt - 1), 0)

    def s_map(i, d):
        return (jnp.maximum(i - pt, 0), 0)

    return pl.pallas_call(
        functools.partial(_combine_kernel, prompt_tiles=pt, total=tp + ts, alpha=alpha),
        grid_spec=pltpu.PrefetchScalarGridSpec(
            num_scalar_prefetch=1,
            grid=(pt + st,),
            in_specs=[pl.BlockSpec((rows, D_MODEL), p_map),
                      pl.BlockSpec((rows, D_MODEL), s_map),
                      pl.BlockSpec((rows, TOP_K), lambda i, d: (i, 0)),
                      pl.BlockSpec(memory_space=pl.ANY),
                      pl.BlockSpec((1, D_MODEL), lambda i, d: (0, 0)),
                      pl.BlockSpec((1, D_MODEL), lambda i, d: (0, 0))],
            out_specs=[pl.BlockSpec((rows, D_MODEL), p_map),
                       pl.BlockSpec((rows, D_MODEL), s_map)],
            scratch_shapes=[pltpu.VMEM((TOP_K, rows, D_MODEL), F32), pltpu.SemaphoreType.DMA]),
        out_shape=[jax.ShapeDtypeStruct((tp, D_MODEL), F32),
                   jax.ShapeDtypeStruct((ts, D_MODEL), F32)],
        compiler_params=pltpu.CompilerParams(dimension_semantics=("arbitrary",)),
        name="combine",
    )(dest_flat, y_p, y_s, gates, out_sorted, ln_g, ln_b)


def _moe(y_p, y_s, w_r, b_r, w1, b1, w2, b2, ln_g, ln_b, *, alpha):
    total = y_p.shape[0] + y_s.shape[0]
    top_e, gates, rank, counts = _router(y_p, y_s, w_r.T, b_r.reshape(N_EXPERTS, 1))

    blk = EXPERT_ROWS
    n_blocks = -(-(total * TOP_K) // blk) + N_EXPERTS
    counts = counts[:, 0]
    padded = (counts + blk - 1) // blk * blk
    pad_end = jnp.cumsum(padded)
    pad_start = pad_end - padded
    dest = (pad_start[top_e] + rank).reshape(TOP_K * total)
    nact = (pad_end[-1] // blk).astype(jnp.int32)
    blk_ids = jnp.minimum(jnp.arange(n_blocks, dtype=jnp.int32), nact - 1)
    blk_e = jnp.minimum(jnp.searchsorted(pad_end, blk_ids * blk, side="right"),
                        N_EXPERTS - 1).astype(jnp.int32)

    xs = _dispatch(dest, y_p, y_s, n_blocks * blk)
    out_sorted = _experts(blk_e, nact.reshape(1), xs, w1.astype(BF16),
                          b1.reshape(N_EXPERTS, 1, 2 * D_FF), w2.astype(BF16),
                          b2.reshape(N_EXPERTS, 1, D_MODEL))
    return _combine(dest, y_p, y_s, gates.T, out_sorted, ln_g.reshape(1, D_MODEL),
                    ln_b.reshape(1, D_MODEL), alpha=alpha)


def kernel(x_prompt, x_sample, state_pool, cache_mem_k, cache_mem_v, mem_prompt, w_in, b_in, ln_v_g, ln_v_b, w_s, b_s, w_a_out, w_pool, ls_pool, w_b_out, w_mk, w_mv, w_m_out, w_o, ln1_g, ln1_b, w_r, b_r, w1, b1, w2, b2, ln2_g, ln2_b):
    depth = w_in.shape[0]
    batch, seq, _ = x_prompt.shape
    nseq, dec, _ = x_sample.shape
    pos0 = 16384
    alpha = (2 * depth) ** 0.25
    assert seq % PROMPT_ROWS == 0 and nseq % SAMPLE_SEQS == 0 and dec <= CHUNK and pos0 % CHUNK == 0
    assert (batch * seq) % ROUTER_ROWS == 0 and (nseq * dec) % ROUTER_ROWS == 0

    y_p = x_prompt.reshape(batch * seq, D_MODEL)
    y_s = x_sample.reshape(nseq * dec, D_MODEL)
    pool_p_out, pool_s_out, mk_out, mv_out, cv_out = [], [], [], [], []
    for l in range(depth):
        mk_p, mv_p = _memproj(mem_prompt.reshape(batch * MEM_LEN, D_MODEL),
                              jnp.concatenate([w_mk[l], w_mv[l]], axis=1).astype(BF16))
        bias = jnp.repeat(b_s[l].T, A_GROUP_DIM, axis=1)
        shared = (w_in[l].astype(BF16), b_in[l].reshape(1, C_IN), ln_v_g[l].reshape(1, A_WIDTH),
                  ln_v_b[l].reshape(1, A_WIDTH))
        tail = (w_a_out[l].astype(BF16), w_pool[l].astype(BF16), ls_pool[l].reshape(1, POOL_WIDTH),
                w_b_out[l].astype(BF16), w_m_out[l].astype(BF16), w_o[l].astype(BF16),
                ln1_g[l].reshape(1, D_MODEL), ln1_b[l].reshape(1, D_MODEL))
        y1_p, pool_p = _mix_prompt(y_p, mk_p, mv_p, shared + (w_s[l], bias) + tail,
                                   batch=batch, seq=seq, alpha=alpha)
        reps = SAMPLE_SEQS
        y1_s, pool_s, v_s = _mix_sample(
            y_s, state_pool[l], cache_mem_k[l].reshape(nseq, MEM_LEN, MEM_WIDTH),
            cache_mem_v[l].reshape(nseq, MEM_LEN, MEM_WIDTH),
            shared + (jnp.tile(w_s[l][:, :dec, :dec], (1, reps, reps)),
                      jnp.tile(bias[:dec], (reps, 1))) + tail,
            nseq=nseq, dec=dec, pos0=pos0, alpha=alpha)
        y_p, y_s = _moe(y1_p, y1_s, w_r[l], b_r[l], w1[l], b1[l], w2[l], b2[l], ln2_g[l], ln2_b[l],
                        alpha=alpha)
        pool_p_out.append(pool_p)
        pool_s_out.append(pool_s)
        mk_out.append(mk_p.reshape(batch, MEM_LEN, MEM_HEADS, MEM_HEAD_DIM))
        mv_out.append(mv_p.reshape(batch, MEM_LEN, MEM_HEADS, MEM_HEAD_DIM))
        cv_out.append(v_s.reshape(nseq, dec, A_WIDTH))
    return (y_p.reshape(batch, seq, D_MODEL), y_s.reshape(nseq, dec, D_MODEL),
            jnp.stack(pool_p_out), jnp.stack(pool_s_out), jnp.stack(mk_out), jnp.stack(mv_out),
            jnp.stack(cv_out))
```

```python
import functools

import jax
import jax.numpy as jnp
from jax import lax
from jax.experimental import pallas as pl
from jax.experimental.pallas import tpu as pltpu

F32 = jnp.float32
BF16 = jnp.bfloat16

D_MODEL = 1024
CHUNK = 128
A_GROUPS = 8
A_GROUP_DIM = 128
A_WIDTH = A_GROUPS * A_GROUP_DIM
POOL_WINDOWS = (2, 4, 8, 16)
POOL_GROUP_DIM = 128
POOL_WIDTH = len(POOL_WINDOWS) * POOL_GROUP_DIM
POOL_HIST = max(POOL_WINDOWS) - 1
MEM_LEN = 256
MEM_HEADS = 4
MEM_HEAD_DIM = 128
MEM_WIDTH = MEM_HEADS * MEM_HEAD_DIM
OFF_P = 2 * A_WIDTH
OFF_Q = OFF_P + POOL_WIDTH
OFF_G = OFF_Q + MEM_WIDTH
C_IN = OFF_G + 3 * D_MODEL
N_EXPERTS = 32
TOP_K = 4
D_FF = D_MODEL
SWIGLU_LIMIT = 7.0
SWIGLU_ALPHA = 1.702
LN_EPS = 1e-5

V7X_VMEM_LIMIT_BYTES = 56 * 1024 * 1024

PROMPT_ROWS = 256
SAMPLE_SEQS = 8
ROUTER_ROWS = 512
MOVE_ROWS = 256
EXPERT_ROWS = 256
HIST_ROWS = 16


def _const_spec(shape):
    nd = len(shape)
    return pl.BlockSpec(shape, lambda *_: (0,) * nd, pipeline_mode=pl.Buffered(1))


def _layer_norm(x, g, b):
    mu = jnp.mean(x, axis=-1, keepdims=True)
    xc = x - mu
    var = jnp.mean(xc * xc, axis=-1, keepdims=True)
    return xc * lax.rsqrt(var + LN_EPS) * g + b


def _dot(a, b):
    return jnp.dot(a, b, preferred_element_type=F32)


def _softmax_rows(s):
    m = jnp.max(s, axis=-1, keepdims=True)
    e = jnp.exp(s - m)
    return e / jnp.sum(e, axis=-1, keepdims=True)


def _memproj_kernel(x_ref, w_ref, k_ref, v_ref):
    y = _dot(x_ref[...].astype(BF16), w_ref[...])
    k_ref[...] = y[:, :MEM_WIDTH]
    v_ref[...] = y[:, MEM_WIDTH:]


def _memproj(mem2d, w_kv):
    rows = mem2d.shape[0]
    blk = 512 if rows % 512 == 0 else MEM_LEN
    return pl.pallas_call(
        _memproj_kernel,
        grid=(rows // blk,),
        in_specs=[pl.BlockSpec((blk, D_MODEL), lambda i: (i, 0)),
                  _const_spec((D_MODEL, 2 * MEM_WIDTH))],
        out_specs=[pl.BlockSpec((blk, MEM_WIDTH), lambda i: (i, 0)),
                   pl.BlockSpec((blk, MEM_WIDTH), lambda i: (i, 0))],
        out_shape=[jax.ShapeDtypeStruct((rows, MEM_WIDTH), F32)] * 2,
        name="memproj",
    )(mem2d, w_kv)


def _project(xb, w_in_ref, b_in_ref, lo, hi):
    return _dot(xb, w_in_ref[:, lo:hi]) + b_in_ref[:, lo:hi]


def _merge_and_norm(x, xb, br_a, br_b, br_m, w_in_ref, b_in_ref, wo_ref, g_ref, b_ref, alpha):
    h = jax.nn.sigmoid(_project(xb, w_in_ref, b_in_ref, OFF_G, OFF_G + D_MODEL)) * br_a
    h += jax.nn.sigmoid(_project(xb, w_in_ref, b_in_ref, OFF_G + D_MODEL, OFF_G + 2 * D_MODEL)) * br_b
    h += jax.nn.sigmoid(_project(xb, w_in_ref, b_in_ref, OFF_G + 2 * D_MODEL, C_IN)) * br_m
    t = _dot(h.astype(BF16), wo_ref[...])
    return _layer_norm(alpha * x + t, g_ref[...], b_ref[...])


def _pool_branch(window_sum, tok, cnt, wpool_ref, ls_ref, wb_ref, rows):
    ys = []
    for g in range(len(POOL_WINDOWS)):
        d = window_sum(g) / cnt(g) - tok(g)
        ys.append(_dot(d.reshape(rows, POOL_GROUP_DIM).astype(BF16), wpool_ref[g]))
    y = jnp.concatenate(ys, axis=1) * ls_ref[...]
    return _dot(y.astype(BF16), wb_ref[...])


def _mix_prompt_kernel(x_ref, mk_ref, mv_ref, w_in_ref, b_in_ref, lnv_g_ref, lnv_b_ref, ws_ref,
                       bs_ref, wa_ref, wpool_ref, ls_ref, wb_ref, wm_ref, wo_ref, ln1_g_ref,
                       ln1_b_ref, y_ref, pool_ref, pbuf, *, alpha):
    j = pl.program_id(1)
    rows = x_ref.shape[0]
    x = x_ref[...]
    xb = x.astype(BF16)

    z = jax.nn.gelu(_project(xb, w_in_ref, b_in_ref, 0, OFF_P))
    u = z[:, :A_WIDTH]
    vn = _layer_norm(z[:, A_WIDTH:], lnv_g_ref[...], lnv_b_ref[...]).astype(BF16)
    tril = (lax.broadcasted_iota(jnp.int32, (CHUNK, CHUNK), 1)
            <= lax.broadcasted_iota(jnp.int32, (CHUNK, CHUNK), 0))
    w_tril = [jnp.where(tril, ws_ref[g], 0.0).astype(BF16) for g in range(A_GROUPS)]
    mixed = []
    for c in range(rows // CHUNK):
        vc = vn[c * CHUNK:(c + 1) * CHUNK]
        cols = [_dot(w_tril[g], vc[:, g * A_GROUP_DIM:(g + 1) * A_GROUP_DIM])
                for g in range(A_GROUPS)]
        mixed.append(jnp.concatenate(cols, axis=1) + bs_ref[...])
    mixed = jnp.concatenate(mixed, axis=0)
    br_a = _dot((u * mixed).astype(BF16), wa_ref[...])

    @pl.when(j == 0)
    def _():
        pbuf[0:HIST_ROWS, :] = jnp.zeros((HIST_ROWS, POOL_WIDTH), F32)

    pbuf[HIST_ROWS:HIST_ROWS + rows, :] = _project(xb, w_in_ref, b_in_ref, OFF_P, OFF_Q)
    pos = j * rows + lax.broadcasted_iota(jnp.int32, (rows, POOL_GROUP_DIM), 0)

    def window_sum(g):
        lo = g * POOL_GROUP_DIM
        acc = pbuf[HIST_ROWS:HIST_ROWS + rows, lo:lo + POOL_GROUP_DIM]
        for i in range(1, POOL_WINDOWS[g]):
            acc = acc + pbuf[HIST_ROWS - i:HIST_ROWS - i + rows, lo:lo + POOL_GROUP_DIM]
        return acc

    def tok(g):
        lo = g * POOL_GROUP_DIM
        return pbuf[HIST_ROWS:HIST_ROWS + rows, lo:lo + POOL_GROUP_DIM]

    def cnt(g):
        return jnp.minimum(POOL_WINDOWS[g], pos + 1).astype(F32)

    br_b = _pool_branch(window_sum, tok, cnt, wpool_ref, ls_ref, wb_ref, rows)
    pool_ref[...] = pbuf[rows + 1:rows + HIST_ROWS, :]
    pbuf[0:HIST_ROWS, :] = pbuf[rows:rows + HIST_ROWS, :]

    q = _project(xb, w_in_ref, b_in_ref, OFF_Q, OFF_G).astype(BF16)
    mk = mk_ref[...].astype(BF16)
    mv = mv_ref[...].astype(BF16)
    outs = []
    for h in range(MEM_HEADS):
        sl = slice(h * MEM_HEAD_DIM, (h + 1) * MEM_HEAD_DIM)
        s = lax.dot_general(q[:, sl], mk[:, sl], (((1,), (1,)), ((), ())),
                            preferred_element_type=F32) * (MEM_HEAD_DIM ** -0.5)
        outs.append(_dot(_softmax_rows(s).astype(BF16), mv[:, sl]))
    br_m = _dot(jnp.concatenate(outs, axis=1).astype(BF16), wm_ref[...])

    y_ref[...] = _merge_and_norm(x, xb, br_a, br_b, br_m, w_in_ref, b_in_ref, wo_ref,
                                 ln1_g_ref, ln1_b_ref, alpha)


def _mix_prompt(x2d, mk, mv, wts, *, batch, seq, alpha):
    rows = PROMPT_ROWS
    steps = seq // rows
    weight_specs = [_const_spec(w.shape) for w in wts]
    return pl.pallas_call(
        functools.partial(_mix_prompt_kernel, alpha=alpha),
        grid=(batch, steps),
        in_specs=[pl.BlockSpec((rows, D_MODEL), lambda n, j: (n * steps + j, 0)),
                  pl.BlockSpec((MEM_LEN, MEM_WIDTH), lambda n, j: (n, 0)),
                  pl.BlockSpec((MEM_LEN, MEM_WIDTH), lambda n, j: (n, 0))] + weight_specs,
        out_specs=[pl.BlockSpec((rows, D_MODEL), lambda n, j: (n * steps + j, 0)),
                   pl.BlockSpec((None, POOL_HIST, POOL_WIDTH), lambda n, j: (n, 0, 0))],
        out_shape=[jax.ShapeDtypeStruct((batch * seq, D_MODEL), F32),
                   jax.ShapeDtypeStruct((batch, POOL_HIST, POOL_WIDTH), F32)],
        scratch_shapes=[pltpu.VMEM((HIST_ROWS + rows, POOL_WIDTH), F32)],
        compiler_params=pltpu.CompilerParams(
            dimension_semantics=("arbitrary", "arbitrary"),
            vmem_limit_bytes=V7X_VMEM_LIMIT_BYTES),
        name="mix_prompt",
    )(x2d, mk, mv, *wts)


def _mix_sample_kernel(x_ref, hist_ref, kc_ref, vc_ref, w_in_ref, b_in_ref, lnv_g_ref, lnv_b_ref,
                       ws_ref, bs_ref, wa_ref, wpool_ref, ls_ref, wb_ref, wm_ref, wo_ref,
                       ln1_g_ref, ln1_b_ref, y_ref, pool_ref, vn_ref, pbuf, *, alpha, pos0, dec):
    rows = x_ref.shape[0]
    seqs = rows // dec
    x = x_ref[...]
    xb = x.astype(BF16)

    z = jax.nn.gelu(_project(xb, w_in_ref, b_in_ref, 0, OFF_P))
    u = z[:, :A_WIDTH]
    vn = _layer_norm(z[:, A_WIDTH:], lnv_g_ref[...], lnv_b_ref[...])
    vn_ref[...] = vn
    vnb = vn.astype(BF16)
    first = dec * lax.broadcasted_iota(jnp.int32, (seqs, dec, rows), 0)
    t = lax.broadcasted_iota(jnp.int32, (seqs, dec, rows), 1)
    c = lax.broadcasted_iota(jnp.int32, (seqs, dec, rows), 2)
    keep = (c >= first) & (c <= first + t)
    cols = []
    for g in range(A_GROUPS):
        wg = jnp.where(keep, ws_ref[g].reshape(seqs, dec, rows), 0.0)
        wg = wg.reshape(rows, rows).astype(BF16)
        cols.append(_dot(wg, vnb[:, g * A_GROUP_DIM:(g + 1) * A_GROUP_DIM]))
    mixed = jnp.concatenate(cols, axis=1) + bs_ref[...]
    br_a = _dot((u * mixed).astype(BF16), wa_ref[...])

    pbuf[:, 1:HIST_ROWS, :] = hist_ref[...]
    pbuf[:, HIST_ROWS:HIST_ROWS + dec, :] = _project(
        xb, w_in_ref, b_in_ref, OFF_P, OFF_Q).reshape(seqs, dec, POOL_WIDTH)
    pos = pos0 + lax.broadcasted_iota(jnp.int32, (seqs, dec, POOL_GROUP_DIM), 1)

    def window_sum(g):
        lo = g * POOL_GROUP_DIM
        acc = pbuf[:, HIST_ROWS:HIST_ROWS + dec, lo:lo + POOL_GROUP_DIM]
        for i in range(1, POOL_WINDOWS[g]):
            acc = acc + pbuf[:, HIST_ROWS - i:HIST_ROWS - i + dec, lo:lo + POOL_GROUP_DIM]
        return acc

    def tok(g):
        lo = g * POOL_GROUP_DIM
        return pbuf[:, HIST_ROWS:HIST_ROWS + dec, lo:lo + POOL_GROUP_DIM]

    def cnt(g):
        return jnp.minimum(POOL_WINDOWS[g], pos + 1).astype(F32)

    br_b = _pool_branch(window_sum, tok, cnt, wpool_ref, ls_ref, wb_ref, rows)
    pool_ref[...] = pbuf[:, dec + 1:dec + HIST_ROWS, :]

    q = _project(xb, w_in_ref, b_in_ref, OFF_Q, OFF_G).astype(BF16)
    outs = []
    for h in range(MEM_HEADS):
        sl = slice(h * MEM_HEAD_DIM, (h + 1) * MEM_HEAD_DIM)
        qh = q[:, sl].reshape(seqs, dec, MEM_HEAD_DIM)
        kh = kc_ref[:, :, sl].astype(BF16)
        vh = vc_ref[:, :, sl].astype(BF16)
        s = jnp.einsum("sld,smd->slm", qh, kh,
                       preferred_element_type=F32) * (MEM_HEAD_DIM ** -0.5)
        o = jnp.einsum("slm,smd->sld", _softmax_rows(s).astype(BF16), vh,
                       preferred_element_type=F32)
        outs.append(o.reshape(rows, MEM_HEAD_DIM))
    br_m = _dot(jnp.concatenate(outs, axis=1).astype(BF16), wm_ref[...])

    y_ref[...] = _merge_and_norm(x, xb, br_a, br_b, br_m, w_in_ref, b_in_ref, wo_ref,
                                 ln1_g_ref, ln1_b_ref, alpha)


def _mix_sample(x2d, hist, kc, vc, wts, *, layer, nseq, dec, pos0, alpha):
    seqs = SAMPLE_SEQS
    rows = seqs * dec
    steps = nseq // seqs
    weight_specs = [_const_spec(w.shape) for w in wts]
    return pl.pallas_call(
        functools.partial(_mix_sample_kernel, alpha=alpha, pos0=pos0, dec=dec),
        grid=(steps,),
        in_specs=[pl.BlockSpec((rows, D_MODEL), lambda i: (i, 0)),
                  pl.BlockSpec((seqs, POOL_HIST, POOL_WIDTH), lambda i: (layer * steps + i, 0, 0)),
                  pl.BlockSpec((seqs, MEM_LEN, MEM_WIDTH), lambda i: (layer * steps + i, 0, 0)),
                  pl.BlockSpec((seqs, MEM_LEN, MEM_WIDTH), lambda i: (layer * steps + i, 0, 0))]
                 + weight_specs,
        out_specs=[pl.BlockSpec((rows, D_MODEL), lambda i: (i, 0)),
                   pl.BlockSpec((seqs, POOL_HIST, POOL_WIDTH), lambda i: (i, 0, 0)),
                   pl.BlockSpec((rows, A_WIDTH), lambda i: (i, 0))],
        out_shape=[jax.ShapeDtypeStruct((nseq * dec, D_MODEL), F32),
                   jax.ShapeDtypeStruct((nseq, POOL_HIST, POOL_WIDTH), F32),
                   jax.ShapeDtypeStruct((nseq * dec, A_WIDTH), F32)],
        scratch_shapes=[pltpu.VMEM((seqs, HIST_ROWS + dec, POOL_WIDTH), F32)],
        compiler_params=pltpu.CompilerParams(
            dimension_semantics=("arbitrary",),
            vmem_limit_bytes=V7X_VMEM_LIMIT_BYTES),
        name="mix_sample",
    )(x2d, hist, kc, vc, *wts)


def _split_bf16(a):
    hi = a.astype(BF16)
    lo = (a - hi.astype(F32)).astype(BF16)
    return hi, lo


def _router_kernel(yp_ref, ys_ref, wr_ref, br_ref, e_ref, g_ref, rank_ref, cnt_ref, carry,
                   *, prompt_tiles):
    i = pl.program_id(0)
    rows = yp_ref.shape[0]

    @pl.when(i == 0)
    def _():
        carry[...] = jnp.zeros(carry.shape, F32)

    x = jnp.where(i < prompt_tiles, yp_ref[...], ys_ref[...])
    x_hi, x_lo = _split_bf16(x)
    w_hi, w_lo = _split_bf16(wr_ref[...])
    nt = (((1,), (1,)), ((), ()))
    logits = (lax.dot_general(w_hi, x_hi, nt, preferred_element_type=F32)
              + lax.dot_general(w_hi, x_lo, nt, preferred_element_type=F32)
              + lax.dot_general(w_lo, x_hi, nt, preferred_element_type=F32)) + br_ref[...]

    eid = lax.broadcasted_iota(jnp.int32, (N_EXPERTS, rows), 0)
    work = logits
    vals, idxs, sels = [], [], []
    for _ in range(TOP_K):
        m = jnp.max(work, axis=0, keepdims=True)
        idx = jnp.min(jnp.where(work == m, eid, N_EXPERTS), axis=0, keepdims=True)
        sel = eid == idx
        work = jnp.where(sel, -jnp.inf, work)
        vals.append(m)
        idxs.append(idx)
        sels.append(sel)
    top_v = jnp.concatenate(vals, axis=0)
    ex = jnp.exp(top_v - top_v[0:1])
    g_ref[...] = ex / jnp.sum(ex, axis=0, keepdims=True)
    e_ref[...] = jnp.concatenate(idxs, axis=0)

    onehot = jnp.zeros((N_EXPERTS, rows), F32)
    for sel in sels:
        onehot = onehot + jnp.where(sel, 1.0, 0.0)
    before = (lax.broadcasted_iota(jnp.int32, (rows, rows), 0)
              < lax.broadcasted_iota(jnp.int32, (rows, rows), 1))
    prefix = _dot(onehot.astype(BF16), jnp.where(before, 1.0, 0.0).astype(BF16)) + carry[...]
    ranks = [jnp.sum(jnp.where(sel, prefix, 0.0), axis=0, keepdims=True) for sel in sels]
    rank_ref[...] = jnp.concatenate(ranks, axis=0).astype(jnp.int32)
    total = carry[...] + jnp.sum(onehot, axis=1, keepdims=True)
    carry[...] = total
    cnt_ref[...] = jnp.broadcast_to(total, cnt_ref.shape).astype(jnp.int32)


def _router(y_p, y_s, w_r_t, b_r_col):
    rows = ROUTER_ROWS
    tp, ts = y_p.shape[0], y_s.shape[0]
    pt, st = tp // rows, ts // rows
    total = tp + ts
    tok_spec = pl.BlockSpec((TOP_K, rows), lambda i: (0, i))
    return pl.pallas_call(
        functools.partial(_router_kernel, prompt_tiles=pt),
        grid=(pt + st,),
        in_specs=[pl.BlockSpec((rows, D_MODEL), lambda i: (jnp.minimum(i, pt - 1), 0)),
                  pl.BlockSpec((rows, D_MODEL), lambda i: (jnp.maximum(i - pt, 0), 0)),
                  _const_spec((N_EXPERTS, D_MODEL)),
                  _const_spec((N_EXPERTS, 1))],
        out_specs=[tok_spec, tok_spec, tok_spec,
                   pl.BlockSpec((N_EXPERTS, 128), lambda i: (0, 0))],
        out_shape=[jax.ShapeDtypeStruct((TOP_K, total), jnp.int32),
                   jax.ShapeDtypeStruct((TOP_K, total), F32),
                   jax.ShapeDtypeStruct((TOP_K, total), jnp.int32),
                   jax.ShapeDtypeStruct((N_EXPERTS, 128), jnp.int32)],
        scratch_shapes=[pltpu.VMEM((N_EXPERTS, 1), F32)],
        compiler_params=pltpu.CompilerParams(dimension_semantics=("arbitrary",)),
        name="router",
    )(y_p, y_s, w_r_t, b_r_col)


def _dispatch_kernel(dest_ref, fill_lo_ref, fill_hi_ref, yp_ref, ys_ref, xs_ref, src, zbuf, sem,
                     *, prompt_tiles, total):
    i = pl.program_id(0)
    rows = src.shape[0]

    @pl.when(i == 0)
    def _():
        zbuf[...] = jnp.zeros(zbuf.shape, F32)

        def zero_row(d):
            return pltpu.make_async_copy(zbuf.at[pl.ds(0, 1)], xs_ref.at[pl.ds(d, 1)], sem)

        def zero_block(b):
            at = pl.multiple_of(b * EXPERT_ROWS, EXPERT_ROWS)
            return pltpu.make_async_copy(zbuf, xs_ref.at[pl.ds(at, EXPERT_ROWS)], sem)

        for e in range(N_EXPERTS):
            lo, hi = fill_lo_ref[e], fill_hi_ref[e]
            lax.fori_loop(lo, hi, lambda d, c: (zero_row(d).start(), c)[1], 0)
            lax.fori_loop(lo, hi, lambda d, c: (zero_row(d).wait(), c)[1], 0)
        lo, hi = fill_hi_ref[N_EXPERTS - 1] // EXPERT_ROWS, xs_ref.shape[0] // EXPERT_ROWS
        lax.fori_loop(lo, hi, lambda b, c: (zero_block(b).start(), c)[1], 0)
        lax.fori_loop(lo, hi, lambda b, c: (zero_block(b).wait(), c)[1], 0)

    src[...] = jnp.where(i < prompt_tiles, yp_ref[...], ys_ref[...])
    base = i * rows

    def row_copy(r, d):
        return pltpu.make_async_copy(src.at[pl.ds(r, 1)], xs_ref.at[pl.ds(d, 1)], sem)

    def issue(r, carry):
        for k in range(TOP_K):
            row_copy(r, dest_ref[k * total + base + r]).start()
        return carry

    def drain(r, carry):
        for k in range(TOP_K):
            row_copy(r, dest_ref[k * total + base + r]).wait()
        return carry

    lax.fori_loop(0, rows, issue, 0)
    lax.fori_loop(0, rows, drain, 0)


def _dispatch(dest_flat, fill_lo, fill_hi, y_p, y_s, n_slots):
    rows = MOVE_ROWS
    tp, ts = y_p.shape[0], y_s.shape[0]
    pt, st = tp // rows, ts // rows
    return pl.pallas_call(
        functools.partial(_dispatch_kernel, prompt_tiles=pt, total=tp + ts),
        grid_spec=pltpu.PrefetchScalarGridSpec(
            num_scalar_prefetch=3,
            grid=(pt + st,),
            in_specs=[pl.BlockSpec((rows, D_MODEL), lambda i, *_: (jnp.minimum(i, pt - 1), 0)),
                      pl.BlockSpec((rows, D_MODEL), lambda i, *_: (jnp.maximum(i - pt, 0), 0))],
            out_specs=pl.BlockSpec(memory_space=pl.ANY),
            scratch_shapes=[pltpu.VMEM((rows, D_MODEL), F32),
                            pltpu.VMEM((EXPERT_ROWS, D_MODEL), F32), pltpu.SemaphoreType.DMA]),
        out_shape=jax.ShapeDtypeStruct((n_slots, D_MODEL), F32),
        compiler_params=pltpu.CompilerParams(dimension_semantics=("arbitrary",)),
        name="dispatch",
    )(dest_flat, fill_lo, fill_hi, y_p, y_s)


def _experts_kernel(blk_e_ref, nact_ref, x_ref, w1_ref, b1_ref, w2_ref, b2_ref, o_ref, w1b, w2b):
    b = pl.program_id(0)
    active = b < nact_ref[0]

    @pl.when(jnp.logical_not(active))
    def _():
        o_ref[...] = jnp.zeros(o_ref.shape, F32)

    @pl.when(active & ((b == 0) | (blk_e_ref[b] != blk_e_ref[jnp.maximum(b - 1, 0)])))
    def _():
        w1b[...] = w1_ref[...].astype(BF16)
        w2b[...] = w2_ref[...].astype(BF16)

    @pl.when(active)
    def _():
        h = _dot(x_ref[...].astype(BF16), w1b[...]) + b1_ref[...]
        g = jnp.minimum(h[:, :D_FF], SWIGLU_LIMIT)
        lin = jnp.clip(h[:, D_FF:], -SWIGLU_LIMIT, SWIGLU_LIMIT)
        act = g * jax.nn.sigmoid(SWIGLU_ALPHA * g) * (lin + 1.0)
        o_ref[...] = _dot(act.astype(BF16), w2b[...]) + b2_ref[...]


def _experts(blk_e, nact, xs, w1, b1, w2, b2):
    rows = EXPERT_ROWS
    n_slots = xs.shape[0]

    def slot_map(b, blk_e, nact):
        return (jnp.minimum(b, nact[0] - 1), 0)

    def expert_map(b, blk_e, nact):
        return (blk_e[b], 0, 0)

    return pl.pallas_call(
        _experts_kernel,
        grid_spec=pltpu.PrefetchScalarGridSpec(
            num_scalar_prefetch=2,
            grid=(n_slots // rows,),
            in_specs=[pl.BlockSpec((rows, D_MODEL), slot_map),
                      pl.BlockSpec((None, D_MODEL, 2 * D_FF), expert_map),
                      pl.BlockSpec((None, 1, 2 * D_FF), expert_map),
                      pl.BlockSpec((None, D_FF, D_MODEL), expert_map),
                      pl.BlockSpec((None, 1, D_MODEL), expert_map)],
            out_specs=pl.BlockSpec((rows, D_MODEL), lambda b, blk_e, nact: (b, 0)),
            scratch_shapes=[pltpu.VMEM((D_MODEL, 2 * D_FF), BF16),
                            pltpu.VMEM((D_FF, D_MODEL), BF16)]),
        out_shape=jax.ShapeDtypeStruct((n_slots, D_MODEL), F32),
        compiler_params=pltpu.CompilerParams(
            dimension_semantics=("arbitrary",),
            vmem_limit_bytes=V7X_VMEM_LIMIT_BYTES),
        name="experts",
    )(blk_e, nact, xs, w1, b1, w2, b2)


def _combine_kernel(dest_ref, yp_ref, ys_ref, g_ref, os_ref, ln_g_ref, ln_b_ref, op_ref, osm_ref,
                    buf, sem, *, prompt_tiles, total, alpha):
    i = pl.program_id(0)
    rows = yp_ref.shape[0]
    base = i * rows

    def row_copy(k, r, d):
        return pltpu.make_async_copy(os_ref.at[pl.ds(d, 1)], buf.at[k, pl.ds(r, 1)], sem)

    def issue(r, carry):
        for k in range(TOP_K):
            row_copy(k, r, dest_ref[k * total + base + r]).start()
        return carry

    def drain(r, carry):
        for k in range(TOP_K):
            row_copy(k, r, dest_ref[k * total + base + r]).wait()
        return carry

    lax.fori_loop(0, rows, issue, 0)
    lax.fori_loop(0, rows, drain, 0)

    y = jnp.where(i < prompt_tiles, yp_ref[...], ys_ref[...])
    g = g_ref[...]
    f = g[:, 0:1] * buf[0]
    for k in range(1, TOP_K):
        f = f + g[:, k:k + 1] * buf[k]
    out = _layer_norm(alpha * y + f, ln_g_ref[...], ln_b_ref[...])

    @pl.when(i < prompt_tiles)
    def _():
        op_ref[...] = out

    @pl.when(i >= prompt_tiles)
    def _():
        osm_ref[...] = out


def _combine(dest_flat, y_p, y_s, gates, out_sorted, ln_g, ln_b, *, alpha):
    rows = MOVE_ROWS
    tp, ts = y_p.shape[0], y_s.shape[0]
    pt, st = tp // rows, ts // rows

    def p_map(i, d):
        return (jnp.minimum(i, pt - 1), 0)

    def s_map(i, d):
        return (jnp.maximum(i - pt, 0), 0)

    return pl.pallas_call(
        functools.partial(_combine_kernel, prompt_tiles=pt, total=tp + ts, alpha=alpha),
        grid_spec=pltpu.PrefetchScalarGridSpec(
            num_scalar_prefetch=1,
            grid=(pt + st,),
            in_specs=[pl.BlockSpec((rows, D_MODEL), p_map),
                      pl.BlockSpec((rows, D_MODEL), s_map),
                      pl.BlockSpec((rows, TOP_K), lambda i, d: (i, 0)),
                      pl.BlockSpec(memory_space=pl.ANY),
                      pl.BlockSpec((1, D_MODEL), lambda i, d: (0, 0)),
                      pl.BlockSpec((1, D_MODEL), lambda i, d: (0, 0))],
            out_specs=[pl.BlockSpec((rows, D_MODEL), p_map),
                       pl.BlockSpec((rows, D_MODEL), s_map)],
            scratch_shapes=[pltpu.VMEM((TOP_K, rows, D_MODEL), F32), pltpu.SemaphoreType.DMA]),
        out_shape=[jax.ShapeDtypeStruct((tp, D_MODEL), F32),
                   jax.ShapeDtypeStruct((ts, D_MODEL), F32)],
        compiler_params=pltpu.CompilerParams(dimension_semantics=("arbitrary",)),
        name="combine",
    )(dest_flat, y_p, y_s, gates, out_sorted, ln_g, ln_b)


def _moe(y_p, y_s, w_r, b_r, w1, b1, w2, b2, ln_g, ln_b, *, layer, alpha):
    total = y_p.shape[0] + y_s.shape[0]
    top_e, gates, rank, counts = _router(y_p, y_s, w_r.T, b_r.reshape(N_EXPERTS, 1))

    blk = EXPERT_ROWS
    n_blocks = -(-(total * TOP_K) // blk) + N_EXPERTS
    counts = counts[:, 0]
    padded = (counts + blk - 1) // blk * blk
    pad_end = jnp.cumsum(padded)
    pad_start = pad_end - padded
    experts = jnp.arange(N_EXPERTS, dtype=jnp.int32)[:, None, None]
    start_of = jnp.sum(jnp.where(top_e[None] == experts, pad_start[:, None, None], 0), axis=0)
    dest = (start_of + rank).reshape(TOP_K * total)
    nact = pad_end[-1] // blk
    blk_ids = jnp.minimum(jnp.arange(n_blocks, dtype=jnp.int32), nact - 1)
    blk_e = jnp.sum((pad_end[None, :] <= (blk_ids * blk)[:, None]).astype(jnp.int32), axis=1)

    xs = _dispatch(dest, pad_start + counts, pad_end, y_p, y_s, n_blocks * blk)
    out_sorted = _experts(blk_e + layer * N_EXPERTS, nact.reshape(1), xs,
                          w1.reshape(-1, D_MODEL, 2 * D_FF), b1.reshape(-1, 1, 2 * D_FF),
                          w2.reshape(-1, D_FF, D_MODEL), b2.reshape(-1, 1, D_MODEL))
    return _combine(dest, y_p, y_s, gates.T, out_sorted, ln_g.reshape(1, D_MODEL),
                    ln_b.reshape(1, D_MODEL), alpha=alpha)


def kernel(x_prompt, x_sample, state_pool, cache_mem_k, cache_mem_v, mem_prompt, w_in, b_in, ln_v_g, ln_v_b, w_s, b_s, w_a_out, w_pool, ls_pool, w_b_out, w_mk, w_mv, w_m_out, w_o, ln1_g, ln1_b, w_r, b_r, w1, b1, w2, b2, ln2_g, ln2_b):
    depth = w_in.shape[0]
    batch, seq, _ = x_prompt.shape
    nseq, dec, _ = x_sample.shape
    pos0 = 16384
    alpha = (2 * depth) ** 0.25
    assert seq % PROMPT_ROWS == 0 and nseq % SAMPLE_SEQS == 0 and dec <= CHUNK and pos0 % CHUNK == 0
    assert (batch * seq) % ROUTER_ROWS == 0 and (nseq * dec) % ROUTER_ROWS == 0

    y_p = x_prompt.reshape(batch * seq, D_MODEL)
    y_s = x_sample.reshape(nseq * dec, D_MODEL)
    pool_p_out, pool_s_out, mk_out, mv_out, cv_out = [], [], [], [], []
    for l in range(depth):
        mk_p, mv_p = _memproj(mem_prompt.reshape(batch * MEM_LEN, D_MODEL),
                              jnp.concatenate([w_mk[l], w_mv[l]], axis=1).astype(BF16))
        bias = jnp.repeat(b_s[l].T, A_GROUP_DIM, axis=1)
        shared = (w_in[l].astype(BF16), b_in[l].reshape(1, C_IN), ln_v_g[l].reshape(1, A_WIDTH),
                  ln_v_b[l].reshape(1, A_WIDTH))
        tail = (w_a_out[l].astype(BF16), w_pool[l].astype(BF16), ls_pool[l].reshape(1, POOL_WIDTH),
                w_b_out[l].astype(BF16), w_m_out[l].astype(BF16), w_o[l].astype(BF16),
                ln1_g[l].reshape(1, D_MODEL), ln1_b[l].reshape(1, D_MODEL))
        y1_p, pool_p = _mix_prompt(y_p, mk_p, mv_p, shared + (w_s[l], bias) + tail,
                                   batch=batch, seq=seq, alpha=alpha)
        reps = SAMPLE_SEQS
        y1_s, pool_s, v_s = _mix_sample(
            y_s, state_pool.reshape(depth * nseq, POOL_HIST, POOL_WIDTH),
            cache_mem_k.reshape(depth * nseq, MEM_LEN, MEM_WIDTH),
            cache_mem_v.reshape(depth * nseq, MEM_LEN, MEM_WIDTH),
            shared + (jnp.tile(w_s[l][:, :dec, :dec], (1, reps, reps)),
                      jnp.tile(bias[:dec], (reps, 1))) + tail,
            layer=l, nseq=nseq, dec=dec, pos0=pos0, alpha=alpha)
        y_p, y_s = _moe(y1_p, y1_s, w_r[l], b_r[l], w1, b1, w2, b2, ln2_g[l], ln2_b[l],
                        layer=l, alpha=alpha)
        pool_p_out.append(pool_p)
        pool_s_out.append(pool_s)
        mk_out.append(mk_p.reshape(batch, MEM_LEN, MEM_HEADS, MEM_HEAD_DIM))
        mv_out.append(mv_p.reshape(batch, MEM_LEN, MEM_HEADS, MEM_HEAD_DIM))
        cv_out.append(v_s.reshape(nseq, dec, A_WIDTH))
    return (y_p.reshape(batch, seq, D_MODEL), y_s.reshape(nseq, dec, D_MODEL),
            jnp.stack(pool_p_out), jnp.stack(pool_s_out), jnp.stack(mk_out), jnp.stack(mv_out),
            jnp.stack(cv_out))
```

```python
import functools

import jax
import jax.numpy as jnp
from jax import lax
from jax.experimental import pallas as pl
from jax.experimental.pallas import tpu as pltpu

F32 = jnp.float32
BF16 = jnp.bfloat16

D_MODEL = 1024
CHUNK = 128
A_GROUPS = 8
A_GROUP_DIM = 128
A_WIDTH = A_GROUPS * A_GROUP_DIM
POOL_WINDOWS = (2, 4, 8, 16)
POOL_GROUP_DIM = 128
POOL_WIDTH = len(POOL_WINDOWS) * POOL_GROUP_DIM
POOL_HIST = max(POOL_WINDOWS) - 1
MEM_LEN = 256
MEM_HEADS = 4
MEM_HEAD_DIM = 128
MEM_WIDTH = MEM_HEADS * MEM_HEAD_DIM
OFF_P = 2 * A_WIDTH
OFF_Q = OFF_P + POOL_WIDTH
OFF_G = OFF_Q + MEM_WIDTH
C_IN = OFF_G + 3 * D_MODEL
N_EXPERTS = 32
TOP_K = 4
D_FF = D_MODEL
SWIGLU_LIMIT = 7.0
SWIGLU_ALPHA = 1.702
LN_EPS = 1e-5

V7X_VMEM_LIMIT_BYTES = 56 * 1024 * 1024

LANES = 128
ROW_TILES = D_MODEL // LANES

PROMPT_ROWS = 256
SAMPLE_SEQS = 8
ROUTER_ROWS = 512
MOVE_ROWS = 256
EXPERT_ROWS = 256
HIST_ROWS = 16


def _const_spec(shape):
    nd = len(shape)
    return pl.BlockSpec(shape, lambda *_: (0,) * nd, pipeline_mode=pl.Buffered(1))


def _layer_norm(x, g, b):
    mu = jnp.mean(x, axis=-1, keepdims=True)
    xc = x - mu
    var = jnp.mean(xc * xc, axis=-1, keepdims=True)
    return xc * lax.rsqrt(var + LN_EPS) * g + b


def _dot(a, b):
    return jnp.dot(a, b, preferred_element_type=F32)


def _softmax_rows(s):
    m = jnp.max(s, axis=-1, keepdims=True)
    e = jnp.exp(s - m)
    return e / jnp.sum(e, axis=-1, keepdims=True)


def _memproj_kernel(x_ref, w_ref, k_ref, v_ref):
    y = _dot(x_ref[...].astype(BF16), w_ref[...])
    k_ref[...] = y[:, :MEM_WIDTH]
    v_ref[...] = y[:, MEM_WIDTH:]


def _memproj(mem2d, w_kv):
    rows = mem2d.shape[0]
    blk = 512 if rows % 512 == 0 else MEM_LEN
    return pl.pallas_call(
        _memproj_kernel,
        grid=(rows // blk,),
        in_specs=[pl.BlockSpec((blk, D_MODEL), lambda i: (i, 0)),
                  _const_spec((D_MODEL, 2 * MEM_WIDTH))],
        out_specs=[pl.BlockSpec((blk, MEM_WIDTH), lambda i: (i, 0)),
                   pl.BlockSpec((blk, MEM_WIDTH), lambda i: (i, 0))],
        out_shape=[jax.ShapeDtypeStruct((rows, MEM_WIDTH), F32)] * 2,
        name="memproj",
    )(mem2d, w_kv)


def _project(xb, w_in_ref, b_in_ref, lo, hi):
    return _dot(xb, w_in_ref[:, lo:hi]) + b_in_ref[:, lo:hi]


def _merge_and_norm(x, xb, br_a, br_b, br_m, w_in_ref, b_in_ref, wo_ref, g_ref, b_ref, alpha):
    h = jax.nn.sigmoid(_project(xb, w_in_ref, b_in_ref, OFF_G, OFF_G + D_MODEL)) * br_a
    h += jax.nn.sigmoid(_project(xb, w_in_ref, b_in_ref, OFF_G + D_MODEL, OFF_G + 2 * D_MODEL)) * br_b
    h += jax.nn.sigmoid(_project(xb, w_in_ref, b_in_ref, OFF_G + 2 * D_MODEL, C_IN)) * br_m
    t = _dot(h.astype(BF16), wo_ref[...])
    return _layer_norm(alpha * x + t, g_ref[...], b_ref[...])


def _pool_branch(window_sum, tok, cnt, wpool_ref, ls_ref, wb_ref, rows):
    ys = []
    for g in range(len(POOL_WINDOWS)):
        d = window_sum(g) / cnt(g) - tok(g)
        ys.append(_dot(d.reshape(rows, POOL_GROUP_DIM).astype(BF16), wpool_ref[g]))
    y = jnp.concatenate(ys, axis=1) * ls_ref[...]
    return _dot(y.astype(BF16), wb_ref[...])


def _mix_prompt_kernel(x_ref, mk_ref, mv_ref, w_in_ref, b_in_ref, lnv_g_ref, lnv_b_ref, ws_ref,
                       bs_ref, wa_ref, wpool_ref, ls_ref, wb_ref, wm_ref, wo_ref, ln1_g_ref,
                       ln1_b_ref, y_ref, pool_ref, pbuf, *, alpha):
    j = pl.program_id(1)
    rows = x_ref.shape[0]
    x = x_ref[...]
    xb = x.astype(BF16)

    z = jax.nn.gelu(_project(xb, w_in_ref, b_in_ref, 0, OFF_P))
    u = z[:, :A_WIDTH]
    vn = _layer_norm(z[:, A_WIDTH:], lnv_g_ref[...], lnv_b_ref[...]).astype(BF16)
    tril = (lax.broadcasted_iota(jnp.int32, (CHUNK, CHUNK), 1)
            <= lax.broadcasted_iota(jnp.int32, (CHUNK, CHUNK), 0))
    w_tril = [jnp.where(tril, ws_ref[g], 0.0).astype(BF16) for g in range(A_GROUPS)]
    mixed = []
    for c in range(rows // CHUNK):
        vc = vn[c * CHUNK:(c + 1) * CHUNK]
        cols = [_dot(w_tril[g], vc[:, g * A_GROUP_DIM:(g + 1) * A_GROUP_DIM])
                for g in range(A_GROUPS)]
        mixed.append(jnp.concatenate(cols, axis=1) + bs_ref[...])
    mixed = jnp.concatenate(mixed, axis=0)
    br_a = _dot((u * mixed).astype(BF16), wa_ref[...])

    @pl.when(j == 0)
    def _():
        pbuf[0:HIST_ROWS, :] = jnp.zeros((HIST_ROWS, POOL_WIDTH), F32)

    pbuf[HIST_ROWS:HIST_ROWS + rows, :] = _project(xb, w_in_ref, b_in_ref, OFF_P, OFF_Q)
    pos = j * rows + lax.broadcasted_iota(jnp.int32, (rows, POOL_GROUP_DIM), 0)

    def window_sum(g):
        lo = g * POOL_GROUP_DIM
        acc = pbuf[HIST_ROWS:HIST_ROWS + rows, lo:lo + POOL_GROUP_DIM]
        for i in range(1, POOL_WINDOWS[g]):
            acc = acc + pbuf[HIST_ROWS - i:HIST_ROWS - i + rows, lo:lo + POOL_GROUP_DIM]
        return acc

    def tok(g):
        lo = g * POOL_GROUP_DIM
        return pbuf[HIST_ROWS:HIST_ROWS + rows, lo:lo + POOL_GROUP_DIM]

    def cnt(g):
        return jnp.minimum(POOL_WINDOWS[g], pos + 1).astype(F32)

    br_b = _pool_branch(window_sum, tok, cnt, wpool_ref, ls_ref, wb_ref, rows)
    pool_ref[...] = pbuf[rows + 1:rows + HIST_ROWS, :]
    pbuf[0:HIST_ROWS, :] = pbuf[rows:rows + HIST_ROWS, :]

    q = _project(xb, w_in_ref, b_in_ref, OFF_Q, OFF_G).astype(BF16)
    mk = mk_ref[...].astype(BF16)
    mv = mv_ref[...].astype(BF16)
    outs = []
    for h in range(MEM_HEADS):
        sl = slice(h * MEM_HEAD_DIM, (h + 1) * MEM_HEAD_DIM)
        s = lax.dot_general(q[:, sl], mk[:, sl], (((1,), (1,)), ((), ())),
                            preferred_element_type=F32) * (MEM_HEAD_DIM ** -0.5)
        outs.append(_dot(_softmax_rows(s).astype(BF16), mv[:, sl]))
    br_m = _dot(jnp.concatenate(outs, axis=1).astype(BF16), wm_ref[...])

    y_ref[...] = _merge_and_norm(x, xb, br_a, br_b, br_m, w_in_ref, b_in_ref, wo_ref,
                                 ln1_g_ref, ln1_b_ref, alpha)


def _mix_prompt(x2d, mk, mv, wts, *, batch, seq, alpha):
    rows = PROMPT_ROWS
    steps = seq // rows
    weight_specs = [_const_spec(w.shape) for w in wts]
    return pl.pallas_call(
        functools.partial(_mix_prompt_kernel, alpha=alpha),
        grid=(batch, steps),
        in_specs=[pl.BlockSpec((rows, D_MODEL), lambda n, j: (n * steps + j, 0)),
                  pl.BlockSpec((MEM_LEN, MEM_WIDTH), lambda n, j: (n, 0)),
                  pl.BlockSpec((MEM_LEN, MEM_WIDTH), lambda n, j: (n, 0))] + weight_specs,
        out_specs=[pl.BlockSpec((rows, D_MODEL), lambda n, j: (n * steps + j, 0)),
                   pl.BlockSpec((None, POOL_HIST, POOL_WIDTH), lambda n, j: (n, 0, 0))],
        out_shape=[jax.ShapeDtypeStruct((batch * seq, D_MODEL), F32),
                   jax.ShapeDtypeStruct((batch, POOL_HIST, POOL_WIDTH), F32)],
        scratch_shapes=[pltpu.VMEM((HIST_ROWS + rows, POOL_WIDTH), F32)],
        compiler_params=pltpu.CompilerParams(
            dimension_semantics=("arbitrary", "arbitrary"),
            vmem_limit_bytes=V7X_VMEM_LIMIT_BYTES),
        name="mix_prompt",
    )(x2d, mk, mv, *wts)


def _mix_sample_kernel(x_ref, hist_ref, kc_ref, vc_ref, w_in_ref, b_in_ref, lnv_g_ref, lnv_b_ref,
                       ws_ref, bs_ref, wa_ref, wpool_ref, ls_ref, wb_ref, wm_ref, wo_ref,
                       ln1_g_ref, ln1_b_ref, y_ref, pool_ref, vn_ref, pbuf, *, alpha, pos0, dec):
    rows = x_ref.shape[0]
    seqs = rows // dec
    x = x_ref[...]
    xb = x.astype(BF16)

    z = jax.nn.gelu(_project(xb, w_in_ref, b_in_ref, 0, OFF_P))
    u = z[:, :A_WIDTH]
    vn = _layer_norm(z[:, A_WIDTH:], lnv_g_ref[...], lnv_b_ref[...])
    vn_ref[...] = vn
    vnb = vn.astype(BF16)
    first = dec * lax.broadcasted_iota(jnp.int32, (seqs, dec, rows), 0)
    t = lax.broadcasted_iota(jnp.int32, (seqs, dec, rows), 1)
    c = lax.broadcasted_iota(jnp.int32, (seqs, dec, rows), 2)
    keep = (c >= first) & (c <= first + t)
    cols = []
    for g in range(A_GROUPS):
        wg = jnp.where(keep, ws_ref[g].reshape(seqs, dec, rows), 0.0)
        wg = wg.reshape(rows, rows).astype(BF16)
        cols.append(_dot(wg, vnb[:, g * A_GROUP_DIM:(g + 1) * A_GROUP_DIM]))
    mixed = jnp.concatenate(cols, axis=1) + bs_ref[...]
    br_a = _dot((u * mixed).astype(BF16), wa_ref[...])

    pbuf[:, 1:HIST_ROWS, :] = hist_ref[...]
    pbuf[:, HIST_ROWS:HIST_ROWS + dec, :] = _project(
        xb, w_in_ref, b_in_ref, OFF_P, OFF_Q).reshape(seqs, dec, POOL_WIDTH)
    pos = pos0 + lax.broadcasted_iota(jnp.int32, (seqs, dec, POOL_GROUP_DIM), 1)

    def window_sum(g):
        lo = g * POOL_GROUP_DIM
        acc = pbuf[:, HIST_ROWS:HIST_ROWS + dec, lo:lo + POOL_GROUP_DIM]
        for i in range(1, POOL_WINDOWS[g]):
            acc = acc + pbuf[:, HIST_ROWS - i:HIST_ROWS - i + dec, lo:lo + POOL_GROUP_DIM]
        return acc

    def tok(g):
        lo = g * POOL_GROUP_DIM
        return pbuf[:, HIST_ROWS:HIST_ROWS + dec, lo:lo + POOL_GROUP_DIM]

    def cnt(g):
        return jnp.minimum(POOL_WINDOWS[g], pos + 1).astype(F32)

    br_b = _pool_branch(window_sum, tok, cnt, wpool_ref, ls_ref, wb_ref, rows)
    pool_ref[...] = pbuf[:, dec + 1:dec + HIST_ROWS, :]

    q = _project(xb, w_in_ref, b_in_ref, OFF_Q, OFF_G).astype(BF16)
    outs = []
    for h in range(MEM_HEADS):
        sl = slice(h * MEM_HEAD_DIM, (h + 1) * MEM_HEAD_DIM)
        qh = q[:, sl].reshape(seqs, dec, MEM_HEAD_DIM)
        kh = kc_ref[:, pl.ds(h, MEM_LEN, stride=MEM_HEADS), :].astype(BF16)
        vh = vc_ref[:, pl.ds(h, MEM_LEN, stride=MEM_HEADS), :].astype(BF16)
        s = jnp.einsum("sld,smd->slm", qh, kh,
                       preferred_element_type=F32) * (MEM_HEAD_DIM ** -0.5)
        o = jnp.einsum("slm,smd->sld", _softmax_rows(s).astype(BF16), vh,
                       preferred_element_type=F32)
        outs.append(o.reshape(rows, MEM_HEAD_DIM))
    br_m = _dot(jnp.concatenate(outs, axis=1).astype(BF16), wm_ref[...])

    y_ref[...] = _merge_and_norm(x, xb, br_a, br_b, br_m, w_in_ref, b_in_ref, wo_ref,
                                 ln1_g_ref, ln1_b_ref, alpha)


def _mix_sample(x2d, hist, kc, vc, wts, *, layer, nseq, dec, pos0, alpha):
    seqs = SAMPLE_SEQS
    rows = seqs * dec
    steps = nseq // seqs
    weight_specs = [_const_spec(w.shape) for w in wts]
    return pl.pallas_call(
        functools.partial(_mix_sample_kernel, alpha=alpha, pos0=pos0, dec=dec),
        grid=(steps,),
        in_specs=[pl.BlockSpec((rows, D_MODEL), lambda i: (i, 0)),
                  pl.BlockSpec((seqs, POOL_HIST, POOL_WIDTH), lambda i: (layer * steps + i, 0, 0)),
                  pl.BlockSpec((seqs, MEM_LEN * MEM_HEADS, MEM_HEAD_DIM),
                               lambda i: (layer * steps + i, 0, 0)),
                  pl.BlockSpec((seqs, MEM_LEN * MEM_HEADS, MEM_HEAD_DIM),
                               lambda i: (layer * steps + i, 0, 0))]
                 + weight_specs,
        out_specs=[pl.BlockSpec((rows, D_MODEL), lambda i: (i, 0)),
                   pl.BlockSpec((seqs, POOL_HIST, POOL_WIDTH), lambda i: (i, 0, 0)),
                   pl.BlockSpec((rows, A_WIDTH), lambda i: (i, 0))],
        out_shape=[jax.ShapeDtypeStruct((nseq * dec, D_MODEL), F32),
                   jax.ShapeDtypeStruct((nseq, POOL_HIST, POOL_WIDTH), F32),
                   jax.ShapeDtypeStruct((nseq * dec, A_WIDTH), F32)],
        scratch_shapes=[pltpu.VMEM((seqs, HIST_ROWS + dec, POOL_WIDTH), F32)],
        compiler_params=pltpu.CompilerParams(
            dimension_semantics=("arbitrary",),
            vmem_limit_bytes=V7X_VMEM_LIMIT_BYTES),
        name="mix_sample",
    )(x2d, hist, kc, vc, *wts)


def _split_bf16(a):
    hi = a.astype(BF16)
    lo = (a - hi.astype(F32)).astype(BF16)
    return hi, lo


def _router_kernel(yp_ref, ys_ref, wr_ref, br_ref, e_ref, g_ref, rank_ref, cnt_ref, carry,
                   *, prompt_tiles):
    i = pl.program_id(0)
    rows = yp_ref.shape[0]

    @pl.when(i == 0)
    def _():
        carry[...] = jnp.zeros(carry.shape, F32)

    x = jnp.where(i < prompt_tiles, yp_ref[...], ys_ref[...])
    x_hi, x_lo = _split_bf16(x)
    w_hi, w_lo = _split_bf16(wr_ref[...])
    nt = (((1,), (1,)), ((), ()))
    logits = (lax.dot_general(w_hi, x_hi, nt, preferred_element_type=F32)
              + lax.dot_general(w_hi, x_lo, nt, preferred_element_type=F32)
              + lax.dot_general(w_lo, x_hi, nt, preferred_element_type=F32)) + br_ref[...]

    eid = lax.broadcasted_iota(jnp.int32, (N_EXPERTS, rows), 0)
    work = logits
    vals, idxs, sels = [], [], []
    for _ in range(TOP_K):
        m = jnp.max(work, axis=0, keepdims=True)
        idx = jnp.min(jnp.where(work == m, eid, N_EXPERTS), axis=0, keepdims=True)
        sel = eid == idx
        work = jnp.where(sel, -jnp.inf, work)
        vals.append(m)
        idxs.append(idx)
        sels.append(sel)
    top_v = jnp.concatenate(vals, axis=0)
    ex = jnp.exp(top_v - top_v[0:1])
    g_ref[...] = ex / jnp.sum(ex, axis=0, keepdims=True)
    e_ref[...] = jnp.concatenate(idxs, axis=0)

    onehot = jnp.zeros((N_EXPERTS, rows), F32)
    for sel in sels:
        onehot = onehot + jnp.where(sel, 1.0, 0.0)
    before = (lax.broadcasted_iota(jnp.int32, (rows, rows), 0)
              < lax.broadcasted_iota(jnp.int32, (rows, rows), 1))
    prefix = _dot(onehot.astype(BF16), jnp.where(before, 1.0, 0.0).astype(BF16)) + carry[...]
    ranks = [jnp.sum(jnp.where(sel, prefix, 0.0), axis=0, keepdims=True) for sel in sels]
    rank_ref[...] = jnp.concatenate(ranks, axis=0).astype(jnp.int32)
    total = carry[...] + jnp.sum(onehot, axis=1, keepdims=True)
    carry[...] = total
    cnt_ref[...] = jnp.broadcast_to(total, cnt_ref.shape).astype(jnp.int32)


def _router(y_p, y_s, w_r_t, b_r_col):
    rows = ROUTER_ROWS
    tp, ts = y_p.shape[0], y_s.shape[0]
    pt, st = tp // rows, ts // rows
    total = tp + ts
    tok_spec = pl.BlockSpec((TOP_K, rows), lambda i: (0, i))
    return pl.pallas_call(
        functools.partial(_router_kernel, prompt_tiles=pt),
        grid=(pt + st,),
        in_specs=[pl.BlockSpec((rows, D_MODEL), lambda i: (jnp.minimum(i, pt - 1), 0)),
                  pl.BlockSpec((rows, D_MODEL), lambda i: (jnp.maximum(i - pt, 0), 0)),
                  _const_spec((N_EXPERTS, D_MODEL)),
                  _const_spec((N_EXPERTS, 1))],
        out_specs=[tok_spec, tok_spec, tok_spec,
                   pl.BlockSpec((N_EXPERTS, 128), lambda i: (0, 0))],
        out_shape=[jax.ShapeDtypeStruct((TOP_K, total), jnp.int32),
                   jax.ShapeDtypeStruct((TOP_K, total), F32),
                   jax.ShapeDtypeStruct((TOP_K, total), jnp.int32),
                   jax.ShapeDtypeStruct((N_EXPERTS, 128), jnp.int32)],
        scratch_shapes=[pltpu.VMEM((N_EXPERTS, 1), F32)],
        compiler_params=pltpu.CompilerParams(dimension_semantics=("arbitrary",)),
        name="router",
    )(y_p, y_s, w_r_t, b_r_col)


def _to_row_tiles(dst, val):
    rows = val.shape[0]
    for c in range(ROW_TILES):
        dst[pl.ds(c, rows, stride=ROW_TILES), :] = val[:, c * LANES:(c + 1) * LANES]


def _from_row_tiles(src, start, rows):
    return jnp.concatenate(
        [src[pl.ds(start + c, rows, stride=ROW_TILES), :] for c in range(ROW_TILES)], axis=1)


def _dispatch_kernel(dest_ref, fill_lo_ref, fill_hi_ref, yp_ref, ys_ref, xs_ref, src, zbuf, sem,
                     *, prompt_tiles, total):
    i = pl.program_id(0)
    rows = yp_ref.shape[0]
    slot = lax.rem(i, 2)

    @pl.when(i == 0)
    def _():
        zbuf[...] = jnp.zeros(zbuf.shape, F32)

        def zero_row(d):
            return pltpu.make_async_copy(zbuf.at[0], xs_ref.at[d], sem.at[2])

        def zero_block(b):
            at = pl.multiple_of(b * EXPERT_ROWS, EXPERT_ROWS)
            return pltpu.make_async_copy(zbuf, xs_ref.at[pl.ds(at, EXPERT_ROWS)], sem.at[2])

        for e in range(N_EXPERTS):
            lax.fori_loop(fill_lo_ref[e], fill_hi_ref[e],
                          lambda d, c: (zero_row(d).start(), c)[1], 0)
        lo, hi = fill_hi_ref[N_EXPERTS - 1] // EXPERT_ROWS, xs_ref.shape[0] // EXPERT_ROWS
        lax.fori_loop(lo, hi, lambda b, c: (zero_block(b).start(), c)[1], 0)
        for e in range(N_EXPERTS):
            lax.fori_loop(fill_lo_ref[e], fill_hi_ref[e],
                          lambda d, c: (zero_row(d).wait(), c)[1], 0)
        lax.fori_loop(lo, hi, lambda b, c: (zero_block(b).wait(), c)[1], 0)

    def row_copy(s, r, d):
        at = pl.multiple_of(r * ROW_TILES, ROW_TILES)
        return pltpu.make_async_copy(src.at[s, pl.ds(at, ROW_TILES)], xs_ref.at[d], sem.at[s])

    def copies(step, s, act):
        def body(r, carry):
            for k in range(TOP_K):
                act(row_copy(s, r, dest_ref[k * total + step * rows + r]))
            return carry
        lax.fori_loop(0, rows, body, 0, unroll=4)

    _to_row_tiles(src.at[slot], jnp.where(i < prompt_tiles, yp_ref[...], ys_ref[...]))
    copies(i, slot, lambda c: c.start())

    @pl.when(i > 0)
    def _():
        copies(i - 1, 1 - slot, lambda c: c.wait())

    @pl.when(i == pl.num_programs(0) - 1)
    def _():
        copies(i, slot, lambda c: c.wait())


def _dispatch(dest_flat, fill_lo, fill_hi, y_p, y_s, n_slots):
    rows = MOVE_ROWS
    tp, ts = y_p.shape[0], y_s.shape[0]
    pt, st = tp // rows, ts // rows
    return pl.pallas_call(
        functools.partial(_dispatch_kernel, prompt_tiles=pt, total=tp + ts),
        grid_spec=pltpu.PrefetchScalarGridSpec(
            num_scalar_prefetch=3,
            grid=(pt + st,),
            in_specs=[pl.BlockSpec((rows, D_MODEL), lambda i, *_: (jnp.minimum(i, pt - 1), 0)),
                      pl.BlockSpec((rows, D_MODEL), lambda i, *_: (jnp.maximum(i - pt, 0), 0))],
            out_specs=pl.BlockSpec(memory_space=pl.ANY),
            scratch_shapes=[pltpu.VMEM((2, rows * ROW_TILES, LANES), F32),
                            pltpu.VMEM((EXPERT_ROWS, ROW_TILES, LANES), F32),
                            pltpu.SemaphoreType.DMA((3,))]),
        out_shape=jax.ShapeDtypeStruct((n_slots, ROW_TILES, LANES), F32),
        compiler_params=pltpu.CompilerParams(dimension_semantics=("arbitrary",)),
        name="dispatch",
    )(dest_flat, fill_lo, fill_hi, y_p, y_s)


def _experts_kernel(blk_e_ref, first_ref, buf_ref, next_ref, nact_ref, x_ref, w1_hbm, b1_ref,
                    w2_hbm, b2_ref, o_ref, wf1, wf2, w1b, w2b, sems):
    b = pl.program_id(0)
    active = b < nact_ref[0]
    rows = x_ref.shape[0] // ROW_TILES

    def fetch(e, half):
        return (pltpu.make_async_copy(w1_hbm.at[e], wf1.at[half], sems.at[0, half]),
                pltpu.make_async_copy(w2_hbm.at[e], wf2.at[half], sems.at[1, half]))

    @pl.when(b == 0)
    def _():
        for c in fetch(blk_e_ref[0], buf_ref[0]):
            c.start()

    @pl.when(jnp.logical_not(active))
    def _():
        o_ref[...] = jnp.zeros(o_ref.shape, F32)

    @pl.when(active & (first_ref[b] == 1))
    def _():
        half = buf_ref[b]
        for c in fetch(blk_e_ref[b], half):
            c.wait()

        @pl.when(next_ref[b] >= 0)
        def _():
            for c in fetch(next_ref[b], 1 - half):
                c.start()

        w1b[...] = wf1[half].astype(BF16)
        w2b[...] = wf2[half].astype(BF16)

    @pl.when(active)
    def _():
        x = _from_row_tiles(x_ref, 0, rows).astype(BF16)
        h = _dot(x, w1b[...]) + b1_ref[...]
        g = jnp.minimum(h[:, :D_FF], SWIGLU_LIMIT)
        lin = jnp.clip(h[:, D_FF:], -SWIGLU_LIMIT, SWIGLU_LIMIT)
        act = g * jax.nn.sigmoid(SWIGLU_ALPHA * g) * (lin + 1.0)
        _to_row_tiles(o_ref, _dot(act.astype(BF16), w2b[...]) + b2_ref[...])


def _experts(blk_e, blk_first, blk_buf, blk_next, nact, xs, w1, b1, w2, b2):
    rows = EXPERT_ROWS
    n_slots = xs.shape[0]

    def slot_map(b, blk_e, first, buf, nxt, nact):
        return (jnp.minimum(b, nact[0] - 1), 0)

    def expert_map(b, blk_e, first, buf, nxt, nact):
        return (blk_e[b], 0, 0)

    out = pl.pallas_call(
        _experts_kernel,
        grid_spec=pltpu.PrefetchScalarGridSpec(
            num_scalar_prefetch=5,
            grid=(n_slots // rows,),
            in_specs=[pl.BlockSpec((rows * ROW_TILES, LANES), slot_map),
                      pl.BlockSpec(memory_space=pl.ANY),
                      pl.BlockSpec((None, 1, 2 * D_FF), expert_map),
                      pl.BlockSpec(memory_space=pl.ANY),
                      pl.BlockSpec((None, 1, D_MODEL), expert_map)],
            out_specs=pl.BlockSpec((rows * ROW_TILES, LANES), lambda b, *_: (b, 0)),
            scratch_shapes=[pltpu.VMEM((2, D_MODEL, 2 * D_FF), F32),
                            pltpu.VMEM((2, D_FF, D_MODEL), F32),
                            pltpu.VMEM((D_MODEL, 2 * D_FF), BF16),
                            pltpu.VMEM((D_FF, D_MODEL), BF16),
                            pltpu.SemaphoreType.DMA((2, 2))]),
        out_shape=jax.ShapeDtypeStruct((n_slots * ROW_TILES, LANES), F32),
        compiler_params=pltpu.CompilerParams(
            dimension_semantics=("arbitrary",),
            vmem_limit_bytes=V7X_VMEM_LIMIT_BYTES),
        name="experts",
    )(blk_e, blk_first, blk_buf, blk_next, nact, xs.reshape(n_slots * ROW_TILES, LANES),
      w1, b1, w2, b2)
    return out.reshape(n_slots, ROW_TILES, LANES)


def _combine_kernel(dest_ref, yp_ref, ys_ref, g_ref, os_ref, ln_g_ref, ln_b_ref, op_ref, osm_ref,
                    buf, sem, *, prompt_tiles, total, alpha):
    i = pl.program_id(0)
    rows = yp_ref.shape[0]
    half = lax.rem(i, 2)

    def row_copy(s, k, r, d):
        at = pl.multiple_of((k * rows + r) * ROW_TILES, ROW_TILES)
        return pltpu.make_async_copy(os_ref.at[d], buf.at[s, pl.ds(at, ROW_TILES)], sem.at[s])

    def copies(step, s, act):
        def body(r, carry):
            for k in range(TOP_K):
                act(row_copy(s, k, r, dest_ref[k * total + step * rows + r]))
            return carry
        lax.fori_loop(0, rows, body, 0, unroll=4)

    @pl.when(i == 0)
    def _():
        copies(0, 0, lambda c: c.start())

    @pl.when(i + 1 < pl.num_programs(0))
    def _():
        copies(i + 1, 1 - half, lambda c: c.start())

    copies(i, half, lambda c: c.wait())

    y = jnp.where(i < prompt_tiles, yp_ref[...], ys_ref[...])
    g = g_ref[...]
    planes = buf.at[half]
    f = g[:, 0:1] * _from_row_tiles(planes, 0, rows)
    for k in range(1, TOP_K):
        f = f + g[:, k:k + 1] * _from_row_tiles(planes, k * rows * ROW_TILES, rows)
    out = _layer_norm(alpha * y + f, ln_g_ref[...], ln_b_ref[...])

    @pl.when(i < prompt_tiles)
    def _():
        op_ref[...] = out

    @pl.when(i >= prompt_tiles)
    def _():
        osm_ref[...] = out


def _combine(dest_flat, y_p, y_s, gates, out_sorted, ln_g, ln_b, *, alpha):
    rows = MOVE_ROWS
    tp, ts = y_p.shape[0], y_s.shape[0]
    pt, st = tp // rows, ts // rows

    def p_map(i, d):
        return (jnp.minimum(i, pt - 1), 0)

    def s_map(i, d):
        return (jnp.maximum(i - pt, 0), 0)

    return pl.pallas_call(
        functools.partial(_combine_kernel, prompt_tiles=pt, total=tp + ts, alpha=alpha),
        grid_spec=pltpu.PrefetchScalarGridSpec(
            num_scalar_prefetch=1,
            grid=(pt + st,),
            in_specs=[pl.BlockSpec((rows, D_MODEL), p_map),
                      pl.BlockSpec((rows, D_MODEL), s_map),
                      pl.BlockSpec((rows, TOP_K), lambda i, d: (i, 0)),
                      pl.BlockSpec(memory_space=pl.ANY),
                      pl.BlockSpec((1, D_MODEL), lambda i, d: (0, 0)),
                      pl.BlockSpec((1, D_MODEL), lambda i, d: (0, 0))],
            out_specs=[pl.BlockSpec((rows, D_MODEL), p_map),
                       pl.BlockSpec((rows, D_MODEL), s_map)],
            scratch_shapes=[pltpu.VMEM((2, TOP_K * rows * ROW_TILES, LANES), F32),
                            pltpu.SemaphoreType.DMA((2,))]),
        out_shape=[jax.ShapeDtypeStruct((tp, D_MODEL), F32),
                   jax.ShapeDtypeStruct((ts, D_MODEL), F32)],
        compiler_params=pltpu.CompilerParams(dimension_semantics=("arbitrary",)),
        name="combine",
    )(dest_flat, y_p, y_s, gates, out_sorted, ln_g, ln_b)


def _moe(y_p, y_s, w_r, b_r, w1, b1, w2, b2, ln_g, ln_b, *, layer, alpha):
    total = y_p.shape[0] + y_s.shape[0]
    top_e, gates, rank, counts = _router(y_p, y_s, w_r.T, b_r.reshape(N_EXPERTS, 1))

    blk = EXPERT_ROWS
    n_blocks = -(-(total * TOP_K) // blk) + N_EXPERTS
    counts = counts[:, 0]
    padded = (counts + blk - 1) // blk * blk
    pad_end = jnp.cumsum(padded)
    pad_start = pad_end - padded
    experts = jnp.arange(N_EXPERTS, dtype=jnp.int32)[:, None, None]
    start_of = jnp.sum(jnp.where(top_e[None] == experts, pad_start[:, None, None], 0), axis=0)
    dest = (start_of + rank).reshape(TOP_K * total)
    nact = pad_end[-1] // blk
    blk_ids = jnp.minimum(jnp.arange(n_blocks, dtype=jnp.int32), nact - 1)
    blk_e = jnp.sum((pad_end[None, :] <= (blk_ids * blk)[:, None]).astype(jnp.int32), axis=1)
    blk_first = jnp.concatenate([jnp.ones((1,), jnp.int32),
                                 (blk_e[1:] != blk_e[:-1]).astype(jnp.int32)])
    blk_buf = (jnp.cumsum(blk_first) - 1) % 2
    ids = jnp.arange(N_EXPERTS, dtype=jnp.int32)
    later_used = (ids[None, :] > ids[:, None]) & (padded[None, :] > 0)
    next_e = jnp.min(jnp.where(later_used, ids[None, :], N_EXPERTS), axis=1)
    next_e = jnp.where(next_e < N_EXPERTS, next_e + layer * N_EXPERTS, -1)
    blk_next = jnp.sum(jnp.where(blk_e[:, None] == ids[None, :], next_e[None, :], 0), axis=1)

    xs = _dispatch(dest, pad_start + counts, pad_end, y_p, y_s, n_blocks * blk)
    out_sorted = _experts(blk_e + layer * N_EXPERTS, blk_first, blk_buf, blk_next, nact.reshape(1),
                          xs, w1.reshape(-1, D_MODEL, 2 * D_FF), b1.reshape(-1, 1, 2 * D_FF),
                          w2.reshape(-1, D_FF, D_MODEL), b2.reshape(-1, 1, D_MODEL))
    return _combine(dest, y_p, y_s, gates.T, out_sorted, ln_g.reshape(1, D_MODEL),
                    ln_b.reshape(1, D_MODEL), alpha=alpha)


def kernel(x_prompt, x_sample, state_pool, cache_mem_k, cache_mem_v, mem_prompt, w_in, b_in, ln_v_g, ln_v_b, w_s, b_s, w_a_out, w_pool, ls_pool, w_b_out, w_mk, w_mv, w_m_out, w_o, ln1_g, ln1_b, w_r, b_r, w1, b1, w2, b2, ln2_g, ln2_b):
    depth = w_in.shape[0]
    batch, seq, _ = x_prompt.shape
    nseq, dec, _ = x_sample.shape
    pos0 = 16384
    alpha = (2 * depth) ** 0.25
    assert seq % PROMPT_ROWS == 0 and nseq % SAMPLE_SEQS == 0 and dec <= CHUNK and pos0 % CHUNK == 0
    assert (batch * seq) % ROUTER_ROWS == 0 and (nseq * dec) % ROUTER_ROWS == 0

    y_p = x_prompt.reshape(batch * seq, D_MODEL)
    y_s = x_sample.reshape(nseq * dec, D_MODEL)
    pool_p_out, pool_s_out, mk_out, mv_out, cv_out = [], [], [], [], []
    for l in range(depth):
        mk_p, mv_p = _memproj(mem_prompt.reshape(batch * MEM_LEN, D_MODEL),
                              jnp.concatenate([w_mk[l], w_mv[l]], axis=1).astype(BF16))
        bias = jnp.repeat(b_s[l].T, A_GROUP_DIM, axis=1)
        shared = (w_in[l].astype(BF16), b_in[l].reshape(1, C_IN), ln_v_g[l].reshape(1, A_WIDTH),
                  ln_v_b[l].reshape(1, A_WIDTH))
        tail = (w_a_out[l].astype(BF16), w_pool[l].astype(BF16), ls_pool[l].reshape(1, POOL_WIDTH),
                w_b_out[l].astype(BF16), w_m_out[l].astype(BF16), w_o[l].astype(BF16),
                ln1_g[l].reshape(1, D_MODEL), ln1_b[l].reshape(1, D_MODEL))
        y1_p, pool_p = _mix_prompt(y_p, mk_p, mv_p, shared + (w_s[l], bias) + tail,
                                   batch=batch, seq=seq, alpha=alpha)
        reps = SAMPLE_SEQS
        y1_s, pool_s, v_s = _mix_sample(
            y_s, state_pool.reshape(depth * nseq, POOL_HIST, POOL_WIDTH),
            cache_mem_k.reshape(depth * nseq, MEM_LEN * MEM_HEADS, MEM_HEAD_DIM),
            cache_mem_v.reshape(depth * nseq, MEM_LEN * MEM_HEADS, MEM_HEAD_DIM),
            shared + (jnp.tile(w_s[l][:, :dec, :dec], (1, reps, reps)),
                      jnp.tile(bias[:dec], (reps, 1))) + tail,
            layer=l, nseq=nseq, dec=dec, pos0=pos0, alpha=alpha)
        y_p, y_s = _moe(y1_p, y1_s, w_r[l], b_r[l], w1, b1, w2, b2, ln2_g[l], ln2_b[l],
                        layer=l, alpha=alpha)
        pool_p_out.append(pool_p)
        pool_s_out.append(pool_s)
        mk_out.append(mk_p.reshape(batch, MEM_LEN, MEM_HEADS, MEM_HEAD_DIM))
        mv_out.append(mv_p.reshape(batch, MEM_LEN, MEM_HEADS, MEM_HEAD_DIM))
        cv_out.append(v_s.reshape(nseq, dec, A_WIDTH))
    return (y_p.reshape(batch, seq, D_MODEL), y_s.reshape(nseq, dec, D_MODEL),
            jnp.stack(pool_p_out), jnp.stack(pool_s_out), jnp.stack(mk_out), jnp.stack(mv_out),
            jnp.stack(cv_out))
```

```python
import functools

import jax
import jax.numpy as jnp
from jax import lax
from jax.experimental import pallas as pl
from jax.experimental.pallas import tpu as pltpu

F32 = jnp.float32
BF16 = jnp.bfloat16

D_MODEL = 1024
CHUNK = 128
A_GROUPS = 8
A_GROUP_DIM = 128
A_WIDTH = A_GROUPS * A_GROUP_DIM
POOL_WINDOWS = (2, 4, 8, 16)
POOL_GROUP_DIM = 128
POOL_WIDTH = len(POOL_WINDOWS) * POOL_GROUP_DIM
POOL_HIST = max(POOL_WINDOWS) - 1
MEM_LEN = 256
MEM_HEADS = 4
MEM_HEAD_DIM = 128
MEM_WIDTH = MEM_HEADS * MEM_HEAD_DIM
OFF_P = 2 * A_WIDTH
OFF_Q = OFF_P + POOL_WIDTH
OFF_G = OFF_Q + MEM_WIDTH
C_IN = OFF_G + 3 * D_MODEL
N_EXPERTS = 32
TOP_K = 4
D_FF = D_MODEL
SWIGLU_LIMIT = 7.0
SWIGLU_ALPHA = 1.702
LN_EPS = 1e-5

V7X_VMEM_LIMIT_BYTES = 56 * 1024 * 1024

LANES = 128
ROW_TILES = D_MODEL // LANES

PROMPT_ROWS = 512
SAMPLE_SEQS = 8
ROUTER_ROWS = 512
MOVE_ROWS = 256
EXPERT_ROWS = 256
HIST_ROWS = 16


def _const_spec(shape):
    nd = len(shape)
    return pl.BlockSpec(shape, lambda *_: (0,) * nd, pipeline_mode=pl.Buffered(1))


def _layer_norm(x, g, b):
    mu = jnp.mean(x, axis=-1, keepdims=True)
    xc = x - mu
    var = jnp.mean(xc * xc, axis=-1, keepdims=True)
    return xc * lax.rsqrt(var + LN_EPS) * g + b


def _dot(a, b):
    return jnp.dot(a, b, preferred_element_type=F32)


def _softmax_rows(s):
    m = jnp.max(s, axis=-1, keepdims=True)
    e = jnp.exp(s - m)
    return e / jnp.sum(e, axis=-1, keepdims=True)


def _memproj_kernel(x_ref, w_ref, k_ref, v_ref):
    y = _dot(x_ref[...].astype(BF16), w_ref[...])
    k_ref[...] = y[:, :MEM_WIDTH]
    v_ref[...] = y[:, MEM_WIDTH:]


def _memproj(mem2d, w_kv):
    rows = mem2d.shape[0]
    blk = 512 if rows % 512 == 0 else MEM_LEN
    return pl.pallas_call(
        _memproj_kernel,
        grid=(rows // blk,),
        in_specs=[pl.BlockSpec((blk, D_MODEL), lambda i: (i, 0)),
                  _const_spec((D_MODEL, 2 * MEM_WIDTH))],
        out_specs=[pl.BlockSpec((blk, MEM_WIDTH), lambda i: (i, 0)),
                   pl.BlockSpec((blk, MEM_WIDTH), lambda i: (i, 0))],
        out_shape=[jax.ShapeDtypeStruct((rows, MEM_WIDTH), F32)] * 2,
        name="memproj",
    )(mem2d, w_kv)


def _project(xb, w_in_ref, b_in_ref, lo, hi):
    return _dot(xb, w_in_ref[:, lo:hi]) + b_in_ref[:, lo:hi]


def _merge_and_norm(x, xb, br_a, br_b, br_m, w_in_ref, b_in_ref, wo_ref, g_ref, b_ref, alpha):
    h = jax.nn.sigmoid(_project(xb, w_in_ref, b_in_ref, OFF_G, OFF_G + D_MODEL)) * br_a
    h += jax.nn.sigmoid(_project(xb, w_in_ref, b_in_ref, OFF_G + D_MODEL, OFF_G + 2 * D_MODEL)) * br_b
    h += jax.nn.sigmoid(_project(xb, w_in_ref, b_in_ref, OFF_G + 2 * D_MODEL, C_IN)) * br_m
    t = _dot(h.astype(BF16), wo_ref[...])
    return _layer_norm(alpha * x + t, g_ref[...], b_ref[...])


def _pool_branch(window_sum, tok, cnt, wpool_ref, ls_ref, wb_ref, rows):
    ys = []
    for g in range(len(POOL_WINDOWS)):
        d = window_sum(g) / cnt(g) - tok(g)
        ys.append(_dot(d.reshape(rows, POOL_GROUP_DIM).astype(BF16), wpool_ref[g]))
    y = jnp.concatenate(ys, axis=1) * ls_ref[...]
    return _dot(y.astype(BF16), wb_ref[...])


def _mix_prompt_kernel(x_ref, mk_ref, mv_ref, w_in_ref, b_in_ref, lnv_g_ref, lnv_b_ref, ws_ref,
                       bs_ref, wa_ref, wpool_ref, ls_ref, wb_ref, wm_ref, wo_ref, ln1_g_ref,
                       ln1_b_ref, y_ref, pool_ref, pbuf, *, alpha):
    j = pl.program_id(1)
    rows = x_ref.shape[0]
    x = x_ref[...]
    xb = x.astype(BF16)

    z = jax.nn.gelu(_project(xb, w_in_ref, b_in_ref, 0, OFF_P))
    u = z[:, :A_WIDTH]
    vn = _layer_norm(z[:, A_WIDTH:], lnv_g_ref[...], lnv_b_ref[...]).astype(BF16)
    tril = (lax.broadcasted_iota(jnp.int32, (CHUNK, CHUNK), 1)
            <= lax.broadcasted_iota(jnp.int32, (CHUNK, CHUNK), 0))
    w_tril = [jnp.where(tril, ws_ref[g], 0.0).astype(BF16) for g in range(A_GROUPS)]
    mixed = []
    for c in range(rows // CHUNK):
        vc = vn[c * CHUNK:(c + 1) * CHUNK]
        cols = [_dot(w_tril[g], vc[:, g * A_GROUP_DIM:(g + 1) * A_GROUP_DIM])
                for g in range(A_GROUPS)]
        mixed.append(jnp.concatenate(cols, axis=1) + bs_ref[...])
    mixed = jnp.concatenate(mixed, axis=0)
    br_a = _dot((u * mixed).astype(BF16), wa_ref[...])

    @pl.when(j == 0)
    def _():
        pbuf[0:HIST_ROWS, :] = jnp.zeros((HIST_ROWS, POOL_WIDTH), F32)

    pbuf[HIST_ROWS:HIST_ROWS + rows, :] = _project(xb, w_in_ref, b_in_ref, OFF_P, OFF_Q)
    pos = j * rows + lax.broadcasted_iota(jnp.int32, (rows, POOL_GROUP_DIM), 0)

    def window_sum(g):
        lo = g * POOL_GROUP_DIM
        acc = pbuf[HIST_ROWS:HIST_ROWS + rows, lo:lo + POOL_GROUP_DIM]
        for i in range(1, POOL_WINDOWS[g]):
            acc = acc + pbuf[HIST_ROWS - i:HIST_ROWS - i + rows, lo:lo + POOL_GROUP_DIM]
        return acc

    def tok(g):
        lo = g * POOL_GROUP_DIM
        return pbuf[HIST_ROWS:HIST_ROWS + rows, lo:lo + POOL_GROUP_DIM]

    def cnt(g):
        return jnp.minimum(POOL_WINDOWS[g], pos + 1).astype(F32)

    br_b = _pool_branch(window_sum, tok, cnt, wpool_ref, ls_ref, wb_ref, rows)
    pool_ref[...] = pbuf[rows + 1:rows + HIST_ROWS, :]
    pbuf[0:HIST_ROWS, :] = pbuf[rows:rows + HIST_ROWS, :]

    q = _project(xb, w_in_ref, b_in_ref, OFF_Q, OFF_G).astype(BF16)
    mk = mk_ref[...].astype(BF16)
    mv = mv_ref[...].astype(BF16)
    outs = []
    for h in range(MEM_HEADS):
        sl = slice(h * MEM_HEAD_DIM, (h + 1) * MEM_HEAD_DIM)
        s = lax.dot_general(q[:, sl], mk[:, sl], (((1,), (1,)), ((), ())),
                            preferred_element_type=F32) * (MEM_HEAD_DIM ** -0.5)
        outs.append(_dot(_softmax_rows(s).astype(BF16), mv[:, sl]))
    br_m = _dot(jnp.concatenate(outs, axis=1).astype(BF16), wm_ref[...])

    y_ref[...] = _merge_and_norm(x, xb, br_a, br_b, br_m, w_in_ref, b_in_ref, wo_ref,
                                 ln1_g_ref, ln1_b_ref, alpha)


def _mix_prompt(x2d, mk, mv, wts, *, batch, seq, alpha):
    rows = PROMPT_ROWS
    steps = seq // rows
    weight_specs = [_const_spec(w.shape) for w in wts]
    return pl.pallas_call(
        functools.partial(_mix_prompt_kernel, alpha=alpha),
        grid=(batch, steps),
        in_specs=[pl.BlockSpec((rows, D_MODEL), lambda n, j: (n * steps + j, 0)),
                  pl.BlockSpec((MEM_LEN, MEM_WIDTH), lambda n, j: (n, 0)),
                  pl.BlockSpec((MEM_LEN, MEM_WIDTH), lambda n, j: (n, 0))] + weight_specs,
        out_specs=[pl.BlockSpec((rows, D_MODEL), lambda n, j: (n * steps + j, 0)),
                   pl.BlockSpec((None, POOL_HIST, POOL_WIDTH), lambda n, j: (n, 0, 0))],
        out_shape=[jax.ShapeDtypeStruct((batch * seq, D_MODEL), F32),
                   jax.ShapeDtypeStruct((batch, POOL_HIST, POOL_WIDTH), F32)],
        scratch_shapes=[pltpu.VMEM((HIST_ROWS + rows, POOL_WIDTH), F32)],
        compiler_params=pltpu.CompilerParams(
            dimension_semantics=("arbitrary", "arbitrary"),
            vmem_limit_bytes=V7X_VMEM_LIMIT_BYTES),
        name="mix_prompt",
    )(x2d, mk, mv, *wts)


def _mix_sample_kernel(x_ref, hist_ref, kc_ref, vc_ref, w_in_ref, b_in_ref, lnv_g_ref, lnv_b_ref,
                       ws_ref, bs_ref, wa_ref, wpool_ref, ls_ref, wb_ref, wm_ref, wo_ref,
                       ln1_g_ref, ln1_b_ref, y_ref, pool_ref, vn_ref, pbuf, *, alpha, pos0, dec):
    rows = x_ref.shape[0]
    seqs = rows // dec
    x = x_ref[...]
    xb = x.astype(BF16)

    z = jax.nn.gelu(_project(xb, w_in_ref, b_in_ref, 0, OFF_P))
    u = z[:, :A_WIDTH]
    vn = _layer_norm(z[:, A_WIDTH:], lnv_g_ref[...], lnv_b_ref[...])
    vn_ref[...] = vn
    vnb = vn.astype(BF16)
    first = dec * lax.broadcasted_iota(jnp.int32, (seqs, dec, rows), 0)
    t = lax.broadcasted_iota(jnp.int32, (seqs, dec, rows), 1)
    c = lax.broadcasted_iota(jnp.int32, (seqs, dec, rows), 2)
    keep = (c >= first) & (c <= first + t)
    cols = []
    for g in range(A_GROUPS):
        wg = jnp.where(keep, ws_ref[g].reshape(seqs, dec, rows), 0.0)
        wg = wg.reshape(rows, rows).astype(BF16)
        cols.append(_dot(wg, vnb[:, g * A_GROUP_DIM:(g + 1) * A_GROUP_DIM]))
    mixed = jnp.concatenate(cols, axis=1) + bs_ref[...]
    br_a = _dot((u * mixed).astype(BF16), wa_ref[...])

    pbuf[:, 1:HIST_ROWS, :] = hist_ref[...]
    pbuf[:, HIST_ROWS:HIST_ROWS + dec, :] = _project(
        xb, w_in_ref, b_in_ref, OFF_P, OFF_Q).reshape(seqs, dec, POOL_WIDTH)
    pos = pos0 + lax.broadcasted_iota(jnp.int32, (seqs, dec, POOL_GROUP_DIM), 1)

    def window_sum(g):
        lo = g * POOL_GROUP_DIM
        acc = pbuf[:, HIST_ROWS:HIST_ROWS + dec, lo:lo + POOL_GROUP_DIM]
        for i in range(1, POOL_WINDOWS[g]):
            acc = acc + pbuf[:, HIST_ROWS - i:HIST_ROWS - i + dec, lo:lo + POOL_GROUP_DIM]
        return acc

    def tok(g):
        lo = g * POOL_GROUP_DIM
        return pbuf[:, HIST_ROWS:HIST_ROWS + dec, lo:lo + POOL_GROUP_DIM]

    def cnt(g):
        return jnp.minimum(POOL_WINDOWS[g], pos + 1).astype(F32)

    br_b = _pool_branch(window_sum, tok, cnt, wpool_ref, ls_ref, wb_ref, rows)
    pool_ref[...] = pbuf[:, dec + 1:dec + HIST_ROWS, :]

    q = _project(xb, w_in_ref, b_in_ref, OFF_Q, OFF_G).astype(BF16)
    outs = []
    for h in range(MEM_HEADS):
        sl = slice(h * MEM_HEAD_DIM, (h + 1) * MEM_HEAD_DIM)
        qh = q[:, sl].reshape(seqs, dec, MEM_HEAD_DIM)
        kh = kc_ref[:, pl.ds(h, MEM_LEN, stride=MEM_HEADS), :].astype(BF16)
        vh = vc_ref[:, pl.ds(h, MEM_LEN, stride=MEM_HEADS), :].astype(BF16)
        s = jnp.einsum("sld,smd->slm", qh, kh,
                       preferred_element_type=F32) * (MEM_HEAD_DIM ** -0.5)
        o = jnp.einsum("slm,smd->sld", _softmax_rows(s).astype(BF16), vh,
                       preferred_element_type=F32)
        outs.append(o.reshape(rows, MEM_HEAD_DIM))
    br_m = _dot(jnp.concatenate(outs, axis=1).astype(BF16), wm_ref[...])

    y_ref[...] = _merge_and_norm(x, xb, br_a, br_b, br_m, w_in_ref, b_in_ref, wo_ref,
                                 ln1_g_ref, ln1_b_ref, alpha)


def _mix_sample(x2d, hist, kc, vc, wts, *, layer, nseq, dec, pos0, alpha):
    seqs = SAMPLE_SEQS
    rows = seqs * dec
    steps = nseq // seqs
    weight_specs = [_const_spec(w.shape) for w in wts]
    return pl.pallas_call(
        functools.partial(_mix_sample_kernel, alpha=alpha, pos0=pos0, dec=dec),
        grid=(steps,),
        in_specs=[pl.BlockSpec((rows, D_MODEL), lambda i: (i, 0)),
                  pl.BlockSpec((seqs, POOL_HIST, POOL_WIDTH), lambda i: (layer * steps + i, 0, 0)),
                  pl.BlockSpec((seqs, MEM_LEN * MEM_HEADS, MEM_HEAD_DIM),
                               lambda i: (layer * steps + i, 0, 0)),
                  pl.BlockSpec((seqs, MEM_LEN * MEM_HEADS, MEM_HEAD_DIM),
                               lambda i: (layer * steps + i, 0, 0))]
                 + weight_specs,
        out_specs=[pl.BlockSpec((rows, D_MODEL), lambda i: (i, 0)),
                   pl.BlockSpec((seqs, POOL_HIST, POOL_WIDTH), lambda i: (i, 0, 0)),
                   pl.BlockSpec((rows, A_WIDTH), lambda i: (i, 0))],
        out_shape=[jax.ShapeDtypeStruct((nseq * dec, D_MODEL), F32),
                   jax.ShapeDtypeStruct((nseq, POOL_HIST, POOL_WIDTH), F32),
                   jax.ShapeDtypeStruct((nseq * dec, A_WIDTH), F32)],
        scratch_shapes=[pltpu.VMEM((seqs, HIST_ROWS + dec, POOL_WIDTH), F32)],
        compiler_params=pltpu.CompilerParams(
            dimension_semantics=("arbitrary",),
            vmem_limit_bytes=V7X_VMEM_LIMIT_BYTES),
        name="mix_sample",
    )(x2d, hist, kc, vc, *wts)


def _split_bf16(a):
    hi = a.astype(BF16)
    lo = (a - hi.astype(F32)).astype(BF16)
    return hi, lo


def _router_kernel(yp_ref, ys_ref, wr_ref, br_ref, e_ref, g_ref, rank_ref, cnt_ref, carry,
                   *, prompt_tiles):
    i = pl.program_id(0)
    rows = yp_ref.shape[0]

    @pl.when(i == 0)
    def _():
        carry[...] = jnp.zeros(carry.shape, F32)

    x = jnp.where(i < prompt_tiles, yp_ref[...], ys_ref[...])
    x_hi, x_lo = _split_bf16(x)
    w_hi, w_lo = _split_bf16(wr_ref[...])
    nt = (((1,), (1,)), ((), ()))
    logits = (lax.dot_general(w_hi, x_hi, nt, preferred_element_type=F32)
              + lax.dot_general(w_hi, x_lo, nt, preferred_element_type=F32)
              + lax.dot_general(w_lo, x_hi, nt, preferred_element_type=F32)) + br_ref[...]

    eid = lax.broadcasted_iota(jnp.int32, (N_EXPERTS, rows), 0)
    work = logits
    vals, idxs, sels = [], [], []
    for _ in range(TOP_K):
        m = jnp.max(work, axis=0, keepdims=True)
        idx = jnp.min(jnp.where(work == m, eid, N_EXPERTS), axis=0, keepdims=True)
        sel = eid == idx
        work = jnp.where(sel, -jnp.inf, work)
        vals.append(m)
        idxs.append(idx)
        sels.append(sel)
    top_v = jnp.concatenate(vals, axis=0)
    ex = jnp.exp(top_v - top_v[0:1])
    g_ref[...] = ex / jnp.sum(ex, axis=0, keepdims=True)
    e_ref[...] = jnp.concatenate(idxs, axis=0)

    onehot = jnp.zeros((N_EXPERTS, rows), F32)
    for sel in sels:
        onehot = onehot + jnp.where(sel, 1.0, 0.0)
    before = (lax.broadcasted_iota(jnp.int32, (rows, rows), 0)
              < lax.broadcasted_iota(jnp.int32, (rows, rows), 1))
    prefix = _dot(onehot.astype(BF16), jnp.where(before, 1.0, 0.0).astype(BF16)) + carry[...]
    ranks = [jnp.sum(jnp.where(sel, prefix, 0.0), axis=0, keepdims=True) for sel in sels]
    rank_ref[...] = jnp.concatenate(ranks, axis=0).astype(jnp.int32)
    total = carry[...] + jnp.sum(onehot, axis=1, keepdims=True)
    carry[...] = total
    cnt_ref[...] = jnp.broadcast_to(total, cnt_ref.shape).astype(jnp.int32)


def _router(y_p, y_s, w_r_t, b_r_col):
    rows = ROUTER_ROWS
    tp, ts = y_p.shape[0], y_s.shape[0]
    pt, st = tp // rows, ts // rows
    total = tp + ts
    tok_spec = pl.BlockSpec((TOP_K, rows), lambda i: (0, i))
    return pl.pallas_call(
        functools.partial(_router_kernel, prompt_tiles=pt),
        grid=(pt + st,),
        in_specs=[pl.BlockSpec((rows, D_MODEL), lambda i: (jnp.minimum(i, pt - 1), 0)),
                  pl.BlockSpec((rows, D_MODEL), lambda i: (jnp.maximum(i - pt, 0), 0)),
                  _const_spec((N_EXPERTS, D_MODEL)),
                  _const_spec((N_EXPERTS, 1))],
        out_specs=[tok_spec, tok_spec, tok_spec,
                   pl.BlockSpec((N_EXPERTS, 128), lambda i: (0, 0))],
        out_shape=[jax.ShapeDtypeStruct((TOP_K, total), jnp.int32),
                   jax.ShapeDtypeStruct((TOP_K, total), F32),
                   jax.ShapeDtypeStruct((TOP_K, total), jnp.int32),
                   jax.ShapeDtypeStruct((N_EXPERTS, 128), jnp.int32)],
        scratch_shapes=[pltpu.VMEM((N_EXPERTS, 1), F32)],
        compiler_params=pltpu.CompilerParams(dimension_semantics=("arbitrary",)),
        name="router",
    )(y_p, y_s, w_r_t, b_r_col)


def _to_row_tiles(dst, val):
    rows = val.shape[0]
    for c in range(ROW_TILES):
        dst[pl.ds(c, rows, stride=ROW_TILES), :] = val[:, c * LANES:(c + 1) * LANES]


def _from_row_tiles(src, start, rows):
    return jnp.concatenate(
        [src[pl.ds(start + c, rows, stride=ROW_TILES), :] for c in range(ROW_TILES)], axis=1)


def _start_alternating(copy, k):
    copy.start(priority=k % 2)


def _dispatch_kernel(dest_ref, fill_lo_ref, fill_hi_ref, yp_ref, ys_ref, xs_ref, src, zbuf, sem,
                     *, prompt_tiles, total):
    i = pl.program_id(0)
    rows = yp_ref.shape[0]
    slot = lax.rem(i, 2)

    @pl.when(i == 0)
    def _():
        zbuf[...] = jnp.zeros(zbuf.shape, F32)

        def zero_row(d):
            return pltpu.make_async_copy(zbuf.at[0], xs_ref.at[d], sem.at[2])

        def zero_block(b):
            at = pl.multiple_of(b * EXPERT_ROWS, EXPERT_ROWS)
            return pltpu.make_async_copy(zbuf, xs_ref.at[pl.ds(at, EXPERT_ROWS)], sem.at[2])

        for e in range(N_EXPERTS):
            lax.fori_loop(fill_lo_ref[e], fill_hi_ref[e],
                          lambda d, c: (zero_row(d).start(), c)[1], 0)
        lo, hi = fill_hi_ref[N_EXPERTS - 1] // EXPERT_ROWS, xs_ref.shape[0] // EXPERT_ROWS
        lax.fori_loop(lo, hi, lambda b, c: (zero_block(b).start(), c)[1], 0)
        for e in range(N_EXPERTS):
            lax.fori_loop(fill_lo_ref[e], fill_hi_ref[e],
                          lambda d, c: (zero_row(d).wait(), c)[1], 0)
        lax.fori_loop(lo, hi, lambda b, c: (zero_block(b).wait(), c)[1], 0)

    def row_copy(s, r, d):
        at = pl.multiple_of(r * ROW_TILES, ROW_TILES)
        return pltpu.make_async_copy(src.at[s, pl.ds(at, ROW_TILES)], xs_ref.at[d], sem.at[s])

    def copies(step, s, act):
        def body(r, carry):
            for k in range(TOP_K):
                act(row_copy(s, r, dest_ref[k * total + step * rows + r]), k)
            return carry
        lax.fori_loop(0, rows, body, 0, unroll=4)

    _to_row_tiles(src.at[slot], jnp.where(i < prompt_tiles, yp_ref[...], ys_ref[...]))
    copies(i, slot, _start_alternating)

    @pl.when(i > 0)
    def _():
        copies(i - 1, 1 - slot, lambda c, k: c.wait())

    @pl.when(i == pl.num_programs(0) - 1)
    def _():
        copies(i, slot, lambda c, k: c.wait())


def _dispatch(dest_flat, fill_lo, fill_hi, y_p, y_s, n_slots):
    rows = MOVE_ROWS
    tp, ts = y_p.shape[0], y_s.shape[0]
    pt, st = tp // rows, ts // rows
    return pl.pallas_call(
        functools.partial(_dispatch_kernel, prompt_tiles=pt, total=tp + ts),
        grid_spec=pltpu.PrefetchScalarGridSpec(
            num_scalar_prefetch=3,
            grid=(pt + st,),
            in_specs=[pl.BlockSpec((rows, D_MODEL), lambda i, *_: (jnp.minimum(i, pt - 1), 0)),
                      pl.BlockSpec((rows, D_MODEL), lambda i, *_: (jnp.maximum(i - pt, 0), 0))],
            out_specs=pl.BlockSpec(memory_space=pl.ANY),
            scratch_shapes=[pltpu.VMEM((2, rows * ROW_TILES, LANES), F32),
                            pltpu.VMEM((EXPERT_ROWS, ROW_TILES, LANES), F32),
                            pltpu.SemaphoreType.DMA((3,))]),
        out_shape=jax.ShapeDtypeStruct((n_slots, ROW_TILES, LANES), F32),
        compiler_params=pltpu.CompilerParams(dimension_semantics=("arbitrary",)),
        name="dispatch",
    )(dest_flat, fill_lo, fill_hi, y_p, y_s)


def _experts_kernel(blk_e_ref, first_ref, buf_ref, next_ref, nact_ref, x_ref, w1_hbm, b1_ref,
                    w2_hbm, b2_ref, o_ref, wf1, wf2, w1b, w2b, sems):
    b = pl.program_id(0)
    active = b < nact_ref[0]
    rows = x_ref.shape[0] // ROW_TILES

    def fetch(e, half):
        return (pltpu.make_async_copy(w1_hbm.at[e], wf1.at[half], sems.at[0, half]),
                pltpu.make_async_copy(w2_hbm.at[e], wf2.at[half], sems.at[1, half]))

    @pl.when(b == 0)
    def _():
        for c in fetch(blk_e_ref[0], buf_ref[0]):
            c.start()

    @pl.when(jnp.logical_not(active))
    def _():
        o_ref[...] = jnp.zeros(o_ref.shape, F32)

    @pl.when(active & (first_ref[b] == 1))
    def _():
        half = buf_ref[b]
        for c in fetch(blk_e_ref[b], half):
            c.wait()

        @pl.when(next_ref[b] >= 0)
        def _():
            for c in fetch(next_ref[b], 1 - half):
                c.start(priority=1)

        w1b[...] = wf1[half].astype(BF16)
        w2b[...] = wf2[half].astype(BF16)

    @pl.when(active)
    def _():
        x = _from_row_tiles(x_ref, 0, rows).astype(BF16)
        h = _dot(x, w1b[...]) + b1_ref[...]
        g = jnp.minimum(h[:, :D_FF], SWIGLU_LIMIT)
        lin = jnp.clip(h[:, D_FF:], -SWIGLU_LIMIT, SWIGLU_LIMIT)
        act = g * jax.nn.sigmoid(SWIGLU_ALPHA * g) * (lin + 1.0)
        _to_row_tiles(o_ref, _dot(act.astype(BF16), w2b[...]) + b2_ref[...])


def _experts(blk_e, blk_first, blk_buf, blk_next, nact, xs, w1, b1, w2, b2):
    rows = EXPERT_ROWS
    n_slots = xs.shape[0]

    def slot_map(b, blk_e, first, buf, nxt, nact):
        return (jnp.minimum(b, nact[0] - 1), 0)

    def expert_map(b, blk_e, first, buf, nxt, nact):
        return (blk_e[b], 0, 0)

    out = pl.pallas_call(
        _experts_kernel,
        grid_spec=pltpu.PrefetchScalarGridSpec(
            num_scalar_prefetch=5,
            grid=(n_slots // rows,),
            in_specs=[pl.BlockSpec((rows * ROW_TILES, LANES), slot_map),
                      pl.BlockSpec(memory_space=pl.ANY),
                      pl.BlockSpec((None, 1, 2 * D_FF), expert_map),
                      pl.BlockSpec(memory_space=pl.ANY),
                      pl.BlockSpec((None, 1, D_MODEL), expert_map)],
            out_specs=pl.BlockSpec((rows * ROW_TILES, LANES), lambda b, *_: (b, 0)),
            scratch_shapes=[pltpu.VMEM((2, D_MODEL, 2 * D_FF), F32),
                            pltpu.VMEM((2, D_FF, D_MODEL), F32),
                            pltpu.VMEM((D_MODEL, 2 * D_FF), BF16),
                            pltpu.VMEM((D_FF, D_MODEL), BF16),
                            pltpu.SemaphoreType.DMA((2, 2))]),
        out_shape=jax.ShapeDtypeStruct((n_slots * ROW_TILES, LANES), F32),
        compiler_params=pltpu.CompilerParams(
            dimension_semantics=("arbitrary",),
            vmem_limit_bytes=V7X_VMEM_LIMIT_BYTES),
        name="experts",
    )(blk_e, blk_first, blk_buf, blk_next, nact, xs.reshape(n_slots * ROW_TILES, LANES),
      w1, b1, w2, b2)
    return out.reshape(n_slots, ROW_TILES, LANES)


def _combine_kernel(dest_ref, y_ref, g_ref, os_ref, ln_g_ref, ln_b_ref, o_ref, buf_a, buf_b, sem,
                    *, first, total, alpha):
    i = pl.program_id(0)
    last = pl.num_programs(0) - 1
    rows = y_ref.shape[0]

    def row_copy(buf, s, k, r, d):
        at = (k * rows + r) * ROW_TILES
        if not isinstance(at, int):
            at = pl.multiple_of(at, ROW_TILES)
        return pltpu.make_async_copy(os_ref.at[d], buf.at[pl.ds(at, ROW_TILES)], sem.at[s])

    def slot_of(step, k, r):
        return dest_ref[k * total + first + step * rows + r]

    def looped(buf, s, step, act):
        def body(r, carry):
            for k in range(TOP_K):
                act(row_copy(buf, s, k, r, slot_of(step, k, r)), k)
            return carry
        lax.fori_loop(0, rows, body, 0, unroll=4)

    def start_unrolled(buf, s, step):
        for r in range(rows):
            for k in range(TOP_K):
                _start_alternating(row_copy(buf, s, k, r, slot_of(step, k, r)), k)

    def finish(buf):
        g = g_ref[...]
        f = g[:, 0:1] * _from_row_tiles(buf, 0, rows)
        for k in range(1, TOP_K):
            f = f + g[:, k:k + 1] * _from_row_tiles(buf, k * rows * ROW_TILES, rows)
        o_ref[...] = _layer_norm(alpha * y_ref[...] + f, ln_g_ref[...], ln_b_ref[...])

    @pl.when(i == 0)
    def _():
        looped(buf_a, 0, 0, _start_alternating)

    ahead = jnp.minimum(i + 1, last)
    for parity, (cur, nxt) in enumerate(((buf_a, buf_b), (buf_b, buf_a))):
        @pl.when(lax.rem(i, 2) == parity)
        def _():
            looped(cur, parity, i, lambda c, k: c.wait())
            start_unrolled(nxt, 1 - parity, ahead)
            finish(cur)

            @pl.when(i == last)
            def _():
                looped(nxt, 1 - parity, ahead, lambda c, k: c.wait())


def _combine(dest_flat, y, gates, out_sorted, ln_g, ln_b, *, first, total, alpha):
    rows = MOVE_ROWS
    off = first // rows
    return pl.pallas_call(
        functools.partial(_combine_kernel, first=first, total=total, alpha=alpha),
        grid_spec=pltpu.PrefetchScalarGridSpec(
            num_scalar_prefetch=1,
            grid=(y.shape[0] // rows,),
            in_specs=[pl.BlockSpec((rows, D_MODEL), lambda i, d: (i, 0)),
                      pl.BlockSpec((rows, TOP_K), lambda i, d: (off + i, 0)),
                      pl.BlockSpec(memory_space=pl.ANY),
                      pl.BlockSpec((1, D_MODEL), lambda i, d: (0, 0)),
                      pl.BlockSpec((1, D_MODEL), lambda i, d: (0, 0))],
            out_specs=pl.BlockSpec((rows, D_MODEL), lambda i, d: (i, 0)),
            scratch_shapes=[pltpu.VMEM((TOP_K * rows * ROW_TILES, LANES), F32),
                            pltpu.VMEM((TOP_K * rows * ROW_TILES, LANES), F32),
                            pltpu.SemaphoreType.DMA((2,))]),
        out_shape=jax.ShapeDtypeStruct(y.shape, F32),
        compiler_params=pltpu.CompilerParams(dimension_semantics=("arbitrary",)),
        name="combine",
    )(dest_flat, y, gates, out_sorted, ln_g, ln_b)


def _moe(y_p, y_s, w_r, b_r, w1, b1, w2, b2, ln_g, ln_b, *, layer, alpha):
    total = y_p.shape[0] + y_s.shape[0]
    top_e, gates, rank, counts = _router(y_p, y_s, w_r.T, b_r.reshape(N_EXPERTS, 1))

    blk = EXPERT_ROWS
    n_blocks = -(-(total * TOP_K) // blk) + N_EXPERTS
    counts = counts[:, 0]
    padded = (counts + blk - 1) // blk * blk
    pad_end = jnp.cumsum(padded)
    pad_start = pad_end - padded
    experts = jnp.arange(N_EXPERTS, dtype=jnp.int32)[:, None, None]
    start_of = jnp.sum(jnp.where(top_e[None] == experts, pad_start[:, None, None], 0), axis=0)
    dest = (start_of + rank).reshape(TOP_K * total)
    nact = pad_end[-1] // blk
    blk_ids = jnp.minimum(jnp.arange(n_blocks, dtype=jnp.int32), nact - 1)
    blk_e = jnp.sum((pad_end[None, :] <= (blk_ids * blk)[:, None]).astype(jnp.int32), axis=1)
    blk_first = jnp.concatenate([jnp.ones((1,), jnp.int32),
                                 (blk_e[1:] != blk_e[:-1]).astype(jnp.int32)])
    blk_buf = (jnp.cumsum(blk_first) - 1) % 2
    ids = jnp.arange(N_EXPERTS, dtype=jnp.int32)
    later_used = (ids[None, :] > ids[:, None]) & (padded[None, :] > 0)
    next_e = jnp.min(jnp.where(later_used, ids[None, :], N_EXPERTS), axis=1)
    next_e = jnp.where(next_e < N_EXPERTS, next_e + layer * N_EXPERTS, -1)
    blk_next = jnp.sum(jnp.where(blk_e[:, None] == ids[None, :], next_e[None, :], 0), axis=1)

    xs = _dispatch(dest, pad_start + counts, pad_end, y_p, y_s, n_blocks * blk)
    out_sorted = _experts(blk_e + layer * N_EXPERTS, blk_first, blk_buf, blk_next, nact.reshape(1),
                          xs, w1.reshape(-1, D_MODEL, 2 * D_FF), b1.reshape(-1, 1, 2 * D_FF),
                          w2.reshape(-1, D_FF, D_MODEL), b2.reshape(-1, 1, D_MODEL))
    gates = gates.T
    ln_g, ln_b = ln_g.reshape(1, D_MODEL), ln_b.reshape(1, D_MODEL)
    return (_combine(dest, y_p, gates, out_sorted, ln_g, ln_b, first=0, total=total, alpha=alpha),
            _combine(dest, y_s, gates, out_sorted, ln_g, ln_b, first=y_p.shape[0], total=total,
                     alpha=alpha))


def kernel(x_prompt, x_sample, state_pool, cache_mem_k, cache_mem_v, mem_prompt, w_in, b_in, ln_v_g, ln_v_b, w_s, b_s, w_a_out, w_pool, ls_pool, w_b_out, w_mk, w_mv, w_m_out, w_o, ln1_g, ln1_b, w_r, b_r, w1, b1, w2, b2, ln2_g, ln2_b):
    depth = w_in.shape[0]
    batch, seq, _ = x_prompt.shape
    nseq, dec, _ = x_sample.shape
    pos0 = 16384
    alpha = (2 * depth) ** 0.25
    assert seq % PROMPT_ROWS == 0 and nseq % SAMPLE_SEQS == 0 and dec <= CHUNK and pos0 % CHUNK == 0
    assert (batch * seq) % ROUTER_ROWS == 0 and (nseq * dec) % ROUTER_ROWS == 0

    y_p = x_prompt.reshape(batch * seq, D_MODEL)
    y_s = x_sample.reshape(nseq * dec, D_MODEL)
    pool_p_out, pool_s_out, mk_out, mv_out, cv_out = [], [], [], [], []
    for l in range(depth):
        mk_p, mv_p = _memproj(mem_prompt.reshape(batch * MEM_LEN, D_MODEL),
                              jnp.concatenate([w_mk[l], w_mv[l]], axis=1).astype(BF16))
        bias = jnp.repeat(b_s[l].T, A_GROUP_DIM, axis=1)
        shared = (w_in[l].astype(BF16), b_in[l].reshape(1, C_IN), ln_v_g[l].reshape(1, A_WIDTH),
                  ln_v_b[l].reshape(1, A_WIDTH))
        tail = (w_a_out[l].astype(BF16), w_pool[l].astype(BF16), ls_pool[l].reshape(1, POOL_WIDTH),
                w_b_out[l].astype(BF16), w_m_out[l].astype(BF16), w_o[l].astype(BF16),
                ln1_g[l].reshape(1, D_MODEL), ln1_b[l].reshape(1, D_MODEL))
        y1_p, pool_p = _mix_prompt(y_p, mk_p, mv_p, shared + (w_s[l], bias) + tail,
                                   batch=batch, seq=seq, alpha=alpha)
        reps = SAMPLE_SEQS
        y1_s, pool_s, v_s = _mix_sample(
            y_s, state_pool.reshape(depth * nseq, POOL_HIST, POOL_WIDTH),
            cache_mem_k.reshape(depth * nseq, MEM_LEN * MEM_HEADS, MEM_HEAD_DIM),
            cache_mem_v.reshape(depth * nseq, MEM_LEN * MEM_HEADS, MEM_HEAD_DIM),
            shared + (jnp.tile(w_s[l][:, :dec, :dec], (1, reps, reps)),
                      jnp.tile(bias[:dec], (reps, 1))) + tail,
            layer=l, nseq=nseq, dec=dec, pos0=pos0, alpha=alpha)
        y_p, y_s = _moe(y1_p, y1_s, w_r[l], b_r[l], w1, b1, w2, b2, ln2_g[l], ln2_b[l],
                        layer=l, alpha=alpha)
        pool_p_out.append(pool_p)
        pool_s_out.append(pool_s)
        mk_out.append(mk_p.reshape(batch, MEM_LEN, MEM_HEADS, MEM_HEAD_DIM))
        mv_out.append(mv_p.reshape(batch, MEM_LEN, MEM_HEADS, MEM_HEAD_DIM))
        cv_out.append(v_s.reshape(nseq, dec, A_WIDTH))
    return (y_p.reshape(batch, seq, D_MODEL), y_s.reshape(nseq, dec, D_MODEL),
            jnp.stack(pool_p_out), jnp.stack(pool_s_out), jnp.stack(mk_out), jnp.stack(mv_out),
            jnp.stack(cv_out))
```

```python
import functools

import jax
import jax.numpy as jnp
from jax import lax
from jax.experimental import pallas as pl
from jax.experimental.pallas import tpu as pltpu

F32 = jnp.float32
BF16 = jnp.bfloat16

D_MODEL = 1024
CHUNK = 128
A_GROUPS = 8
A_GROUP_DIM = 128
A_WIDTH = A_GROUPS * A_GROUP_DIM
POOL_WINDOWS = (2, 4, 8, 16)
POOL_GROUP_DIM = 128
POOL_WIDTH = len(POOL_WINDOWS) * POOL_GROUP_DIM
POOL_HIST = max(POOL_WINDOWS) - 1
MEM_LEN = 256
MEM_HEADS = 4
MEM_HEAD_DIM = 128
MEM_WIDTH = MEM_HEADS * MEM_HEAD_DIM
OFF_P = 2 * A_WIDTH
OFF_Q = OFF_P + POOL_WIDTH
OFF_G = OFF_Q + MEM_WIDTH
C_IN = OFF_G + 3 * D_MODEL
N_EXPERTS = 32
TOP_K = 4
D_FF = D_MODEL
SWIGLU_LIMIT = 7.0
SWIGLU_ALPHA = 1.702
LN_EPS = 1e-5

V7X_VMEM_LIMIT_BYTES = 56 * 1024 * 1024

LANES = 128
ROW_TILES = D_MODEL // LANES

PROMPT_ROWS = 512
SAMPLE_SEQS = 8
ROUTER_ROWS = 512
MOVE_ROWS = 256
EXPERT_ROWS = 512
HIST_ROWS = 16


def _const_spec(shape):
    nd = len(shape)
    return pl.BlockSpec(shape, lambda *_: (0,) * nd, pipeline_mode=pl.Buffered(1))


def _layer_norm(x, g, b):
    mu = jnp.mean(x, axis=-1, keepdims=True)
    xc = x - mu
    var = jnp.mean(xc * xc, axis=-1, keepdims=True)
    return xc * lax.rsqrt(var + LN_EPS) * g + b


def _dot(a, b):
    return jnp.dot(a, b, preferred_element_type=F32)


def _softmax_rows(s):
    m = jnp.max(s, axis=-1, keepdims=True)
    e = jnp.exp(s - m)
    return e / jnp.sum(e, axis=-1, keepdims=True)


def _memproj_kernel(x_ref, w_ref, k_ref, v_ref):
    y = _dot(x_ref[...].astype(BF16), w_ref[...])
    k_ref[...] = y[:, :MEM_WIDTH]
    v_ref[...] = y[:, MEM_WIDTH:]


def _memproj(mem2d, w_kv):
    rows = mem2d.shape[0]
    blk = 512 if rows % 512 == 0 else MEM_LEN
    return pl.pallas_call(
        _memproj_kernel,
        grid=(rows // blk,),
        in_specs=[pl.BlockSpec((blk, D_MODEL), lambda i: (i, 0)),
                  _const_spec((D_MODEL, 2 * MEM_WIDTH))],
        out_specs=[pl.BlockSpec((blk, MEM_WIDTH), lambda i: (i, 0)),
                   pl.BlockSpec((blk, MEM_WIDTH), lambda i: (i, 0))],
        out_shape=[jax.ShapeDtypeStruct((rows, MEM_WIDTH), F32)] * 2,
        name="memproj",
    )(mem2d, w_kv)


def _project(xb, w_in_ref, b_in_ref, lo, hi):
    return _dot(xb, w_in_ref[:, lo:hi]) + b_in_ref[:, lo:hi]


def _branch_gates(xb, w_in_ref, b_in_ref):
    return [jax.nn.sigmoid(_project(xb, w_in_ref, b_in_ref, OFF_G + k * D_MODEL,
                                    OFF_G + (k + 1) * D_MODEL)) for k in range(3)]


def _merge_and_norm(x, gates, branches, wo_ref, g_ref, b_ref, alpha):
    h = gates[0] * branches[0] + gates[1] * branches[1] + gates[2] * branches[2]
    t = _dot(h.astype(BF16), wo_ref[...])
    return _layer_norm(alpha * x + t, g_ref[...], b_ref[...])


def _pool_branch(window_sum, tok, cnt, wpool_ref, ls_ref, wb_ref, rows):
    ys = []
    for g in range(len(POOL_WINDOWS)):
        d = window_sum(g) / cnt(g) - tok(g)
        ys.append(_dot(d.reshape(rows, POOL_GROUP_DIM).astype(BF16), wpool_ref[g]))
    y = jnp.concatenate(ys, axis=1) * ls_ref[...]
    return _dot(y.astype(BF16), wb_ref[...])


def _mix_prompt_kernel(x_ref, mk_ref, mv_ref, w_in_ref, b_in_ref, lnv_g_ref, lnv_b_ref, ws_ref,
                       bs_ref, wa_ref, wpool_ref, ls_ref, wb_ref, wm_ref, wo_ref, ln1_g_ref,
                       ln1_b_ref, y_ref, pool_ref, pbuf, *, alpha):
    j = pl.program_id(1)
    rows = x_ref.shape[0]
    x = x_ref[...]
    xb = x.astype(BF16)

    z = jax.nn.gelu(_project(xb, w_in_ref, b_in_ref, 0, OFF_P))
    u = z[:, :A_WIDTH]
    vn = _layer_norm(z[:, A_WIDTH:], lnv_g_ref[...], lnv_b_ref[...]).astype(BF16)
    tril = (lax.broadcasted_iota(jnp.int32, (CHUNK, CHUNK), 1)
            <= lax.broadcasted_iota(jnp.int32, (CHUNK, CHUNK), 0))
    w_tril = [jnp.where(tril, ws_ref[g], 0.0).astype(BF16) for g in range(A_GROUPS)]
    mixed = []
    for c in range(rows // CHUNK):
        vc = vn[c * CHUNK:(c + 1) * CHUNK]
        cols = [_dot(w_tril[g], vc[:, g * A_GROUP_DIM:(g + 1) * A_GROUP_DIM])
                for g in range(A_GROUPS)]
        mixed.append(jnp.concatenate(cols, axis=1) + bs_ref[...])
    mixed = jnp.concatenate(mixed, axis=0)
    br_a = _dot((u * mixed).astype(BF16), wa_ref[...])

    @pl.when(j == 0)
    def _():
        pbuf[0:HIST_ROWS, :] = jnp.zeros((HIST_ROWS, POOL_WIDTH), F32)

    pbuf[HIST_ROWS:HIST_ROWS + rows, :] = _project(xb, w_in_ref, b_in_ref, OFF_P, OFF_Q)
    pos = j * rows + lax.broadcasted_iota(jnp.int32, (rows, POOL_GROUP_DIM), 0)

    def window_sum(g):
        lo = g * POOL_GROUP_DIM
        acc = pbuf[HIST_ROWS:HIST_ROWS + rows, lo:lo + POOL_GROUP_DIM]
        for i in range(1, POOL_WINDOWS[g]):
            acc = acc + pbuf[HIST_ROWS - i:HIST_ROWS - i + rows, lo:lo + POOL_GROUP_DIM]
        return acc

    def tok(g):
        lo = g * POOL_GROUP_DIM
        return pbuf[HIST_ROWS:HIST_ROWS + rows, lo:lo + POOL_GROUP_DIM]

    def cnt(g):
        return jnp.minimum(POOL_WINDOWS[g], pos + 1).astype(F32)

    br_b = _pool_branch(window_sum, tok, cnt, wpool_ref, ls_ref, wb_ref, rows)
    pool_ref[...] = pbuf[rows + 1:rows + HIST_ROWS, :]
    pbuf[0:HIST_ROWS, :] = pbuf[rows:rows + HIST_ROWS, :]

    q = _project(xb, w_in_ref, b_in_ref, OFF_Q, OFF_G).astype(BF16)
    mk = mk_ref[...].astype(BF16)
    mv = mv_ref[...].astype(BF16)
    outs = []
    for h in range(MEM_HEADS):
        sl = slice(h * MEM_HEAD_DIM, (h + 1) * MEM_HEAD_DIM)
        s = lax.dot_general(q[:, sl], mk[:, sl], (((1,), (1,)), ((), ())),
                            preferred_element_type=F32) * (MEM_HEAD_DIM ** -0.5)
        outs.append(_dot(_softmax_rows(s).astype(BF16), mv[:, sl]))
    br_m = _dot(jnp.concatenate(outs, axis=1).astype(BF16), wm_ref[...])

    gates = _branch_gates(xb, w_in_ref, b_in_ref)
    y_ref[...] = _merge_and_norm(x, gates, (br_a, br_b, br_m), wo_ref, ln1_g_ref, ln1_b_ref,
                                 alpha)


def _mix_prompt(x2d, mk, mv, wts, *, batch, seq, alpha):
    rows = PROMPT_ROWS
    steps = seq // rows
    weight_specs = [_const_spec(w.shape) for w in wts]
    return pl.pallas_call(
        functools.partial(_mix_prompt_kernel, alpha=alpha),
        grid=(batch, steps),
        in_specs=[pl.BlockSpec((rows, D_MODEL), lambda n, j: (n * steps + j, 0)),
                  pl.BlockSpec((MEM_LEN, MEM_WIDTH), lambda n, j: (n, 0)),
                  pl.BlockSpec((MEM_LEN, MEM_WIDTH), lambda n, j: (n, 0))] + weight_specs,
        out_specs=[pl.BlockSpec((rows, D_MODEL), lambda n, j: (n * steps + j, 0)),
                   pl.BlockSpec((None, POOL_HIST, POOL_WIDTH), lambda n, j: (n, 0, 0))],
        out_shape=[jax.ShapeDtypeStruct((batch * seq, D_MODEL), F32),
                   jax.ShapeDtypeStruct((batch, POOL_HIST, POOL_WIDTH), F32)],
        scratch_shapes=[pltpu.VMEM((HIST_ROWS + rows, POOL_WIDTH), F32)],
        compiler_params=pltpu.CompilerParams(
            dimension_semantics=("arbitrary", "arbitrary"),
            vmem_limit_bytes=V7X_VMEM_LIMIT_BYTES),
        name="mix_prompt",
    )(x2d, mk, mv, *wts)


def _mix_sample_kernel(x_ref, hist_ref, kc_ref, vc_ref, w_in_ref, b_in_ref, lnv_g_ref, lnv_b_ref,
                       ws_ref, bs_ref, wa_ref, wpool_ref, ls_ref, wb_ref, wm_ref, wo_ref,
                       ln1_g_ref, ln1_b_ref, y_ref, pool_ref, vn_ref, pbuf, *, alpha, pos0, dec):
    rows = x_ref.shape[0]
    seqs = rows // dec
    x = x_ref[...]
    xb = x.astype(BF16)

    z = jax.nn.gelu(_project(xb, w_in_ref, b_in_ref, 0, OFF_P))
    u = z[:, :A_WIDTH]
    vn = _layer_norm(z[:, A_WIDTH:], lnv_g_ref[...], lnv_b_ref[...])
    vn_ref[...] = vn
    vnb = vn.astype(BF16)
    first = dec * lax.broadcasted_iota(jnp.int32, (seqs, dec, rows), 0)
    t = lax.broadcasted_iota(jnp.int32, (seqs, dec, rows), 1)
    c = lax.broadcasted_iota(jnp.int32, (seqs, dec, rows), 2)
    keep = (c >= first) & (c <= first + t)
    cols = []
    for g in range(A_GROUPS):
        wg = jnp.where(keep, ws_ref[g].reshape(seqs, dec, rows), 0.0)
        wg = wg.reshape(rows, rows).astype(BF16)
        cols.append(_dot(wg, vnb[:, g * A_GROUP_DIM:(g + 1) * A_GROUP_DIM]))
    mixed = jnp.concatenate(cols, axis=1) + bs_ref[...]
    br_a = _dot((u * mixed).astype(BF16), wa_ref[...])

    pbuf[:, 1:HIST_ROWS, :] = hist_ref[...]
    pbuf[:, HIST_ROWS:HIST_ROWS + dec, :] = _project(
        xb, w_in_ref, b_in_ref, OFF_P, OFF_Q).reshape(seqs, dec, POOL_WIDTH)
    pos = pos0 + lax.broadcasted_iota(jnp.int32, (seqs, dec, POOL_GROUP_DIM), 1)

    def window_sum(g):
        lo = g * POOL_GROUP_DIM
        acc = pbuf[:, HIST_ROWS:HIST_ROWS + dec, lo:lo + POOL_GROUP_DIM]
        for i in range(1, POOL_WINDOWS[g]):
            acc = acc + pbuf[:, HIST_ROWS - i:HIST_ROWS - i + dec, lo:lo + POOL_GROUP_DIM]
        return acc

    def tok(g):
        lo = g * POOL_GROUP_DIM
        return pbuf[:, HIST_ROWS:HIST_ROWS + dec, lo:lo + POOL_GROUP_DIM]

    def cnt(g):
        return jnp.minimum(POOL_WINDOWS[g], pos + 1).astype(F32)

    br_b = _pool_branch(window_sum, tok, cnt, wpool_ref, ls_ref, wb_ref, rows)
    pool_ref[...] = pbuf[:, dec + 1:dec + HIST_ROWS, :]

    q = _project(xb, w_in_ref, b_in_ref, OFF_Q, OFF_G).astype(BF16)
    outs = []
    for h in range(MEM_HEADS):
        sl = slice(h * MEM_HEAD_DIM, (h + 1) * MEM_HEAD_DIM)
        qh = q[:, sl].reshape(seqs, dec, MEM_HEAD_DIM)
        kh = kc_ref[:, pl.ds(h, MEM_LEN, stride=MEM_HEADS), :].astype(BF16)
        vh = vc_ref[:, pl.ds(h, MEM_LEN, stride=MEM_HEADS), :].astype(BF16)
        s = jnp.einsum("sld,smd->slm", qh, kh,
                       preferred_element_type=F32) * (MEM_HEAD_DIM ** -0.5)
        o = jnp.einsum("slm,smd->sld", _softmax_rows(s).astype(BF16), vh,
                       preferred_element_type=F32)
        outs.append(o.reshape(rows, MEM_HEAD_DIM))
    br_m = _dot(jnp.concatenate(outs, axis=1).astype(BF16), wm_ref[...])

    gates = _branch_gates(xb, w_in_ref, b_in_ref)
    y_ref[...] = _merge_and_norm(x, gates, (br_a, br_b, br_m), wo_ref, ln1_g_ref, ln1_b_ref,
                                 alpha)


def _mix_sample(x2d, hist, kc, vc, wts, *, layer, nseq, dec, pos0, alpha):
    seqs = SAMPLE_SEQS
    rows = seqs * dec
    steps = nseq // seqs
    weight_specs = [_const_spec(w.shape) for w in wts]
    return pl.pallas_call(
        functools.partial(_mix_sample_kernel, alpha=alpha, pos0=pos0, dec=dec),
        grid=(steps,),
        in_specs=[pl.BlockSpec((rows, D_MODEL), lambda i: (i, 0)),
                  pl.BlockSpec((seqs, POOL_HIST, POOL_WIDTH), lambda i: (layer * steps + i, 0, 0)),
                  pl.BlockSpec((seqs, MEM_LEN * MEM_HEADS, MEM_HEAD_DIM),
                               lambda i: (layer * steps + i, 0, 0)),
                  pl.BlockSpec((seqs, MEM_LEN * MEM_HEADS, MEM_HEAD_DIM),
                               lambda i: (layer * steps + i, 0, 0))]
                 + weight_specs,
        out_specs=[pl.BlockSpec((rows, D_MODEL), lambda i: (i, 0)),
                   pl.BlockSpec((seqs, POOL_HIST, POOL_WIDTH), lambda i: (i, 0, 0)),
                   pl.BlockSpec((rows, A_WIDTH), lambda i: (i, 0))],
        out_shape=[jax.ShapeDtypeStruct((nseq * dec, D_MODEL), F32),
                   jax.ShapeDtypeStruct((nseq, POOL_HIST, POOL_WIDTH), F32),
                   jax.ShapeDtypeStruct((nseq * dec, A_WIDTH), F32)],
        scratch_shapes=[pltpu.VMEM((seqs, HIST_ROWS + dec, POOL_WIDTH), F32)],
        compiler_params=pltpu.CompilerParams(
            dimension_semantics=("arbitrary",),
            vmem_limit_bytes=V7X_VMEM_LIMIT_BYTES),
        name="mix_sample",
    )(x2d, hist, kc, vc, *wts)


def _split_bf16(a):
    hi = a.astype(BF16)
    lo = (a - hi.astype(F32)).astype(BF16)
    return hi, lo


def _router_kernel(yp_ref, ys_ref, wr_ref, br_ref, e_ref, g_ref, rank_ref, cnt_ref, carry,
                   *, prompt_tiles):
    i = pl.program_id(0)
    rows = yp_ref.shape[0]

    @pl.when(i == 0)
    def _():
        carry[...] = jnp.zeros(carry.shape, F32)

    x = jnp.where(i < prompt_tiles, yp_ref[...], ys_ref[...])
    x_hi, x_lo = _split_bf16(x)
    w_hi, w_lo = _split_bf16(wr_ref[...])
    nt = (((1,), (1,)), ((), ()))
    logits = (lax.dot_general(w_hi, x_hi, nt, preferred_element_type=F32)
              + lax.dot_general(w_hi, x_lo, nt, preferred_element_type=F32)
              + lax.dot_general(w_lo, x_hi, nt, preferred_element_type=F32)) + br_ref[...]

    eid = lax.broadcasted_iota(jnp.int32, (N_EXPERTS, rows), 0)
    work = logits
    vals, idxs, sels = [], [], []
    for _ in range(TOP_K):
        m = jnp.max(work, axis=0, keepdims=True)
        idx = jnp.min(jnp.where(work == m, eid, N_EXPERTS), axis=0, keepdims=True)
        sel = eid == idx
        work = jnp.where(sel, -jnp.inf, work)
        vals.append(m)
        idxs.append(idx)
        sels.append(sel)
    top_v = jnp.concatenate(vals, axis=0)
    ex = jnp.exp(top_v - top_v[0:1])
    g_ref[...] = ex / jnp.sum(ex, axis=0, keepdims=True)
    e_ref[...] = jnp.concatenate(idxs, axis=0)

    onehot = jnp.zeros((N_EXPERTS, rows), F32)
    for sel in sels:
        onehot = onehot + jnp.where(sel, 1.0, 0.0)
    before = (lax.broadcasted_iota(jnp.int32, (rows, rows), 0)
              < lax.broadcasted_iota(jnp.int32, (rows, rows), 1))
    prefix = _dot(onehot.astype(BF16), jnp.where(before, 1.0, 0.0).astype(BF16)) + carry[...]
    ranks = [jnp.sum(jnp.where(sel, prefix, 0.0), axis=0, keepdims=True) for sel in sels]
    rank_ref[...] = jnp.concatenate(ranks, axis=0).astype(jnp.int32)
    total = carry[...] + jnp.sum(onehot, axis=1, keepdims=True)
    carry[...] = total
    cnt_ref[...] = jnp.broadcast_to(total, cnt_ref.shape).astype(jnp.int32)


def _router(y_p, y_s, w_r_t, b_r_col):
    rows = ROUTER_ROWS
    tp, ts = y_p.shape[0], y_s.shape[0]
    pt, st = tp // rows, ts // rows
    total = tp + ts
    tok_spec = pl.BlockSpec((TOP_K, rows), lambda i: (0, i))
    return pl.pallas_call(
        functools.partial(_router_kernel, prompt_tiles=pt),
        grid=(pt + st,),
        in_specs=[pl.BlockSpec((rows, D_MODEL), lambda i: (jnp.minimum(i, pt - 1), 0)),
                  pl.BlockSpec((rows, D_MODEL), lambda i: (jnp.maximum(i - pt, 0), 0)),
                  _const_spec((N_EXPERTS, D_MODEL)),
                  _const_spec((N_EXPERTS, 1))],
        out_specs=[tok_spec, tok_spec, tok_spec,
                   pl.BlockSpec((N_EXPERTS, 128), lambda i: (0, 0))],
        out_shape=[jax.ShapeDtypeStruct((TOP_K, total), jnp.int32),
                   jax.ShapeDtypeStruct((TOP_K, total), F32),
                   jax.ShapeDtypeStruct((TOP_K, total), jnp.int32),
                   jax.ShapeDtypeStruct((N_EXPERTS, 128), jnp.int32)],
        scratch_shapes=[pltpu.VMEM((N_EXPERTS, 1), F32)],
        compiler_params=pltpu.CompilerParams(dimension_semantics=("arbitrary",)),
        name="router",
    )(y_p, y_s, w_r_t, b_r_col)


def _to_row_tiles(dst, val):
    rows = val.shape[0]
    for c in range(ROW_TILES):
        dst[pl.ds(c, rows, stride=ROW_TILES), :] = val[:, c * LANES:(c + 1) * LANES]


def _from_row_tiles(src, start, rows):
    return jnp.concatenate(
        [src[pl.ds(start + c, rows, stride=ROW_TILES), :] for c in range(ROW_TILES)], axis=1)


def _start_alternating(copy, k):
    copy.start(priority=k % 2)


def _dispatch_kernel(dest_ref, fill_lo_ref, fill_hi_ref, yp_ref, ys_ref, xs_ref, src, zbuf, sem,
                     *, prompt_tiles, total):
    i = pl.program_id(0)
    rows = yp_ref.shape[0]
    slot = lax.rem(i, 2)

    @pl.when(i == 0)
    def _():
        zbuf[...] = jnp.zeros(zbuf.shape, F32)

        def zero_block(b):
            at = pl.multiple_of(b * EXPERT_ROWS, EXPERT_ROWS)
            return pltpu.make_async_copy(zbuf, xs_ref.at[pl.ds(at, EXPERT_ROWS)], sem.at[2])

        def zero_padding(e, act):
            at, left = fill_lo_ref[e], fill_hi_ref[e] - fill_lo_ref[e]
            size = EXPERT_ROWS // 2
            while size:
                has = (left & size) != 0

                @pl.when(has)
                def _(at=at, size=size):
                    act(pltpu.make_async_copy(zbuf.at[pl.ds(0, size)],
                                              xs_ref.at[pl.ds(at, size)], sem.at[2]))

                at = at + jnp.where(has, size, 0)
                size //= 2

        def each_expert(act):
            lax.fori_loop(0, N_EXPERTS, lambda e, c: (zero_padding(e, act), c)[1], 0)

        lo, hi = fill_hi_ref[N_EXPERTS - 1] // EXPERT_ROWS, xs_ref.shape[0] // EXPERT_ROWS
        each_expert(lambda c: c.start())
        lax.fori_loop(lo, hi, lambda b, c: (zero_block(b).start(), c)[1], 0)
        each_expert(lambda c: c.wait())
        lax.fori_loop(lo, hi, lambda b, c: (zero_block(b).wait(), c)[1], 0)

    def row_copy(s, r, d):
        at = r * ROW_TILES
        if not isinstance(at, int):
            at = pl.multiple_of(at, ROW_TILES)
        return pltpu.make_async_copy(src.at[s, pl.ds(at, ROW_TILES)], xs_ref.at[d], sem.at[s])

    def copies(step, s, act):
        def body(r, carry):
            for k in range(TOP_K):
                act(row_copy(s, r, dest_ref[k * total + step * rows + r]), k)
            return carry
        lax.fori_loop(0, rows, body, 0, unroll=4)

    _to_row_tiles(src.at[slot], jnp.where(i < prompt_tiles, yp_ref[...], ys_ref[...]))
    for r in range(rows):
        for k in range(TOP_K):
            _start_alternating(row_copy(slot, r, dest_ref[k * total + i * rows + r]), k)

    @pl.when(i > 0)
    def _():
        copies(i - 1, 1 - slot, lambda c, k: c.wait())

    @pl.when(i == pl.num_programs(0) - 1)
    def _():
        copies(i, slot, lambda c, k: c.wait())


def _dispatch(dest_flat, fill_lo, fill_hi, y_p, y_s, n_slots):
    rows = MOVE_ROWS
    tp, ts = y_p.shape[0], y_s.shape[0]
    pt, st = tp // rows, ts // rows
    return pl.pallas_call(
        functools.partial(_dispatch_kernel, prompt_tiles=pt, total=tp + ts),
        grid_spec=pltpu.PrefetchScalarGridSpec(
            num_scalar_prefetch=3,
            grid=(pt + st,),
            in_specs=[pl.BlockSpec((rows, D_MODEL), lambda i, *_: (jnp.minimum(i, pt - 1), 0)),
                      pl.BlockSpec((rows, D_MODEL), lambda i, *_: (jnp.maximum(i - pt, 0), 0))],
            out_specs=pl.BlockSpec(memory_space=pl.ANY),
            scratch_shapes=[pltpu.VMEM((2, rows * ROW_TILES, LANES), F32),
                            pltpu.VMEM((EXPERT_ROWS, ROW_TILES, LANES), F32),
                            pltpu.SemaphoreType.DMA((3,))]),
        out_shape=jax.ShapeDtypeStruct((n_slots, ROW_TILES, LANES), F32),
        compiler_params=pltpu.CompilerParams(dimension_semantics=("arbitrary",)),
        name="dispatch",
    )(dest_flat, fill_lo, fill_hi, y_p, y_s)


def _experts_kernel(blk_e_ref, first_ref, buf_ref, next_ref, nact_ref, x_ref, w1_hbm, b1_ref,
                    w2_hbm, b2_ref, o_ref, wf1, wf2, w1b, w2b, sems):
    b = pl.program_id(0)
    active = b < nact_ref[0]
    rows = x_ref.shape[0] // ROW_TILES

    def fetch(e, half):
        return (pltpu.make_async_copy(w1_hbm.at[e], wf1.at[half], sems.at[0, half]),
                pltpu.make_async_copy(w2_hbm.at[e], wf2.at[half], sems.at[1, half]))

    @pl.when(b == 0)
    def _():
        for c in fetch(blk_e_ref[0], buf_ref[0]):
            c.start()

    @pl.when(jnp.logical_not(active))
    def _():
        o_ref[...] = jnp.zeros(o_ref.shape, F32)

    @pl.when(active & (first_ref[b] == 1))
    def _():
        half = buf_ref[b]
        for c in fetch(blk_e_ref[b], half):
            c.wait()

        @pl.when(next_ref[b] >= 0)
        def _():
            for c in fetch(next_ref[b], 1 - half):
                c.start(priority=1)

        w1b[...] = wf1[half].astype(BF16)
        w2b[...] = wf2[half].astype(BF16)

    @pl.when(active)
    def _():
        x = _from_row_tiles(x_ref, 0, rows).astype(BF16)
        h = _dot(x, w1b[...]) + b1_ref[...]
        g = jnp.minimum(h[:, :D_FF], SWIGLU_LIMIT)
        lin = jnp.clip(h[:, D_FF:], -SWIGLU_LIMIT, SWIGLU_LIMIT)
        act = g * jax.nn.sigmoid(SWIGLU_ALPHA * g) * (lin + 1.0)
        _to_row_tiles(o_ref, _dot(act.astype(BF16), w2b[...]) + b2_ref[...])


def _experts(blk_e, blk_first, blk_buf, blk_next, nact, xs, w1, b1, w2, b2):
    rows = EXPERT_ROWS
    n_slots = xs.shape[0]

    def slot_map(b, blk_e, first, buf, nxt, nact):
        return (jnp.minimum(b, nact[0] - 1), 0)

    def expert_map(b, blk_e, first, buf, nxt, nact):
        return (blk_e[b], 0, 0)

    out = pl.pallas_call(
        _experts_kernel,
        grid_spec=pltpu.PrefetchScalarGridSpec(
            num_scalar_prefetch=5,
            grid=(n_slots // rows,),
            in_specs=[pl.BlockSpec((rows * ROW_TILES, LANES), slot_map),
                      pl.BlockSpec(memory_space=pl.ANY),
                      pl.BlockSpec((None, 1, 2 * D_FF), expert_map),
                      pl.BlockSpec(memory_space=pl.ANY),
                      pl.BlockSpec((None, 1, D_MODEL), expert_map)],
            out_specs=pl.BlockSpec((rows * ROW_TILES, LANES), lambda b, *_: (b, 0)),
            scratch_shapes=[pltpu.VMEM((2, D_MODEL, 2 * D_FF), F32),
                            pltpu.VMEM((2, D_FF, D_MODEL), F32),
                            pltpu.VMEM((D_MODEL, 2 * D_FF), BF16),
                            pltpu.VMEM((D_FF, D_MODEL), BF16),
                            pltpu.SemaphoreType.DMA((2, 2))]),
        out_shape=jax.ShapeDtypeStruct((n_slots * ROW_TILES, LANES), F32),
        compiler_params=pltpu.CompilerParams(
            dimension_semantics=("arbitrary",),
            vmem_limit_bytes=V7X_VMEM_LIMIT_BYTES),
        name="experts",
    )(blk_e, blk_first, blk_buf, blk_next, nact, xs.reshape(n_slots * ROW_TILES, LANES),
      w1, b1, w2, b2)
    return out.reshape(n_slots, ROW_TILES, LANES)


def _combine_kernel(dest_ref, y_ref, g_ref, os_ref, ln_g_ref, ln_b_ref, o_ref, buf_a, buf_b, sem,
                    *, first, total, alpha):
    i = pl.program_id(0)
    last = pl.num_programs(0) - 1
    rows = y_ref.shape[0]

    def row_copy(buf, s, k, r, d):
        at = (k * rows + r) * ROW_TILES
        if not isinstance(at, int):
            at = pl.multiple_of(at, ROW_TILES)
        return pltpu.make_async_copy(os_ref.at[d], buf.at[pl.ds(at, ROW_TILES)], sem.at[s])

    def slot_of(step, k, r):
        return dest_ref[k * total + first + step * rows + r]

    def looped(buf, s, step, act):
        def body(r, carry):
            for k in range(TOP_K):
                act(row_copy(buf, s, k, r, slot_of(step, k, r)), k)
            return carry
        lax.fori_loop(0, rows, body, 0, unroll=4)

    def start_unrolled(buf, s, step):
        for r in range(rows):
            for k in range(TOP_K):
                _start_alternating(row_copy(buf, s, k, r, slot_of(step, k, r)), k)

    def finish(buf):
        g = g_ref[...]
        f = g[:, 0:1] * _from_row_tiles(buf, 0, rows)
        for k in range(1, TOP_K):
            f = f + g[:, k:k + 1] * _from_row_tiles(buf, k * rows * ROW_TILES, rows)
        o_ref[...] = _layer_norm(alpha * y_ref[...] + f, ln_g_ref[...], ln_b_ref[...])

    @pl.when(i == 0)
    def _():
        looped(buf_a, 0, 0, _start_alternating)

    ahead = jnp.minimum(i + 1, last)
    for parity, (cur, nxt) in enumerate(((buf_a, buf_b), (buf_b, buf_a))):
        @pl.when(lax.rem(i, 2) == parity)
        def _():
            looped(cur, parity, i, lambda c, k: c.wait())
            start_unrolled(nxt, 1 - parity, ahead)
            finish(cur)

            @pl.when(i == last)
            def _():
                looped(nxt, 1 - parity, ahead, lambda c, k: c.wait())


def _combine(dest_flat, y, gates, out_sorted, ln_g, ln_b, *, first, total, alpha):
    rows = MOVE_ROWS
    off = first // rows
    return pl.pallas_call(
        functools.partial(_combine_kernel, first=first, total=total, alpha=alpha),
        grid_spec=pltpu.PrefetchScalarGridSpec(
            num_scalar_prefetch=1,
            grid=(y.shape[0] // rows,),
            in_specs=[pl.BlockSpec((rows, D_MODEL), lambda i, d: (i, 0)),
                      pl.BlockSpec((rows, TOP_K), lambda i, d: (off + i, 0)),
                      pl.BlockSpec(memory_space=pl.ANY),
                      pl.BlockSpec((1, D_MODEL), lambda i, d: (0, 0)),
                      pl.BlockSpec((1, D_MODEL), lambda i, d: (0, 0))],
            out_specs=pl.BlockSpec((rows, D_MODEL), lambda i, d: (i, 0)),
            scratch_shapes=[pltpu.VMEM((TOP_K * rows * ROW_TILES, LANES), F32),
                            pltpu.VMEM((TOP_K * rows * ROW_TILES, LANES), F32),
                            pltpu.SemaphoreType.DMA((2,))]),
        out_shape=jax.ShapeDtypeStruct(y.shape, F32),
        compiler_params=pltpu.CompilerParams(dimension_semantics=("arbitrary",)),
        name="combine",
    )(dest_flat, y, gates, out_sorted, ln_g, ln_b)


def _moe(y_p, y_s, w_r, b_r, w1, b1, w2, b2, ln_g, ln_b, *, layer, alpha):
    total = y_p.shape[0] + y_s.shape[0]
    top_e, gates, rank, counts = _router(y_p, y_s, w_r.T, b_r.reshape(N_EXPERTS, 1))

    blk = EXPERT_ROWS
    n_blocks = -(-(total * TOP_K) // blk) + N_EXPERTS
    counts = counts[:, 0]
    padded = (counts + blk - 1) // blk * blk
    pad_end = jnp.cumsum(padded)
    pad_start = pad_end - padded
    experts = jnp.arange(N_EXPERTS, dtype=jnp.int32)[:, None, None]
    start_of = jnp.sum(jnp.where(top_e[None] == experts, pad_start[:, None, None], 0), axis=0)
    dest = (start_of + rank).reshape(TOP_K * total)
    nact = pad_end[-1] // blk
    blk_ids = jnp.minimum(jnp.arange(n_blocks, dtype=jnp.int32), nact - 1)
    blk_e = jnp.sum((pad_end[None, :] <= (blk_ids * blk)[:, None]).astype(jnp.int32), axis=1)
    blk_first = jnp.concatenate([jnp.ones((1,), jnp.int32),
                                 (blk_e[1:] != blk_e[:-1]).astype(jnp.int32)])
    blk_buf = (jnp.cumsum(blk_first) - 1) % 2
    ids = jnp.arange(N_EXPERTS, dtype=jnp.int32)
    later_used = (ids[None, :] > ids[:, None]) & (padded[None, :] > 0)
    next_e = jnp.min(jnp.where(later_used, ids[None, :], N_EXPERTS), axis=1)
    next_e = jnp.where(next_e < N_EXPERTS, next_e + layer * N_EXPERTS, -1)
    blk_next = jnp.sum(jnp.where(blk_e[:, None] == ids[None, :], next_e[None, :], 0), axis=1)

    xs = _dispatch(dest, pad_start + counts, pad_end, y_p, y_s, n_blocks * blk)
    out_sorted = _experts(blk_e + layer * N_EXPERTS, blk_first, blk_buf, blk_next, nact.reshape(1),
                          xs, w1.reshape(-1, D_MODEL, 2 * D_FF), b1.reshape(-1, 1, 2 * D_FF),
                          w2.reshape(-1, D_FF, D_MODEL), b2.reshape(-1, 1, D_MODEL))
    gates = gates.T
    ln_g, ln_b = ln_g.reshape(1, D_MODEL), ln_b.reshape(1, D_MODEL)
    return (_combine(dest, y_p, gates, out_sorted, ln_g, ln_b, first=0, total=total, alpha=alpha),
            _combine(dest, y_s, gates, out_sorted, ln_g, ln_b, first=y_p.shape[0], total=total,
                     alpha=alpha))


def kernel(x_prompt, x_sample, state_pool, cache_mem_k, cache_mem_v, mem_prompt, w_in, b_in, ln_v_g, ln_v_b, w_s, b_s, w_a_out, w_pool, ls_pool, w_b_out, w_mk, w_mv, w_m_out, w_o, ln1_g, ln1_b, w_r, b_r, w1, b1, w2, b2, ln2_g, ln2_b):
    depth = w_in.shape[0]
    batch, seq, _ = x_prompt.shape
    nseq, dec, _ = x_sample.shape
    pos0 = 16384
    alpha = (2 * depth) ** 0.25
    assert seq % PROMPT_ROWS == 0 and nseq % SAMPLE_SEQS == 0 and dec <= CHUNK and pos0 % CHUNK == 0
    assert (batch * seq) % ROUTER_ROWS == 0 and (nseq * dec) % ROUTER_ROWS == 0

    y_p = x_prompt.reshape(batch * seq, D_MODEL)
    y_s = x_sample.reshape(nseq * dec, D_MODEL)
    pool_p_out, pool_s_out, mk_out, mv_out, cv_out = [], [], [], [], []
    for l in range(depth):
        mk_p, mv_p = _memproj(mem_prompt.reshape(batch * MEM_LEN, D_MODEL),
                              jnp.concatenate([w_mk[l], w_mv[l]], axis=1).astype(BF16))
        bias = jnp.repeat(b_s[l].T, A_GROUP_DIM, axis=1)
        shared = (w_in[l].astype(BF16), b_in[l].reshape(1, C_IN), ln_v_g[l].reshape(1, A_WIDTH),
                  ln_v_b[l].reshape(1, A_WIDTH))
        tail = (w_a_out[l].astype(BF16), w_pool[l].astype(BF16), ls_pool[l].reshape(1, POOL_WIDTH),
                w_b_out[l].astype(BF16), w_m_out[l].astype(BF16), w_o[l].astype(BF16),
                ln1_g[l].reshape(1, D_MODEL), ln1_b[l].reshape(1, D_MODEL))
        y1_p, pool_p = _mix_prompt(y_p, mk_p, mv_p, shared + (w_s[l], bias) + tail,
                                   batch=batch, seq=seq, alpha=alpha)
        reps = SAMPLE_SEQS
        y1_s, pool_s, v_s = _mix_sample(
            y_s, state_pool.reshape(depth * nseq, POOL_HIST, POOL_WIDTH),
            cache_mem_k.reshape(depth * nseq, MEM_LEN * MEM_HEADS, MEM_HEAD_DIM),
            cache_mem_v.reshape(depth * nseq, MEM_LEN * MEM_HEADS, MEM_HEAD_DIM),
            shared + (jnp.tile(w_s[l][:, :dec, :dec], (1, reps, reps)),
                      jnp.tile(bias[:dec], (reps, 1))) + tail,
            layer=l, nseq=nseq, dec=dec, pos0=pos0, alpha=alpha)
        y_p, y_s = _moe(y1_p, y1_s, w_r[l], b_r[l], w1, b1, w2, b2, ln2_g[l], ln2_b[l],
                        layer=l, alpha=alpha)
        pool_p_out.append(pool_p)
        pool_s_out.append(pool_s)
        mk_out.append(mk_p.reshape(batch, MEM_LEN, MEM_HEADS, MEM_HEAD_DIM))
        mv_out.append(mv_p.reshape(batch, MEM_LEN, MEM_HEADS, MEM_HEAD_DIM))
        cv_out.append(v_s.reshape(nseq, dec, A_WIDTH))
    return (y_p.reshape(batch, seq, D_MODEL), y_s.reshape(nseq, dec, D_MODEL),
            jnp.stack(pool_p_out), jnp.stack(pool_s_out), jnp.stack(mk_out), jnp.stack(mv_out),
            jnp.stack(cv_out))
```

```python
import functools

import jax
import jax.numpy as jnp
from jax import lax
from jax.experimental import pallas as pl
from jax.experimental.pallas import tpu as pltpu

F32 = jnp.float32
BF16 = jnp.bfloat16

D_MODEL = 1024
CHUNK = 128
A_GROUPS = 8
A_GROUP_DIM = 128
A_WIDTH = A_GROUPS * A_GROUP_DIM
POOL_WINDOWS = (2, 4, 8, 16)
POOL_GROUP_DIM = 128
POOL_WIDTH = len(POOL_WINDOWS) * POOL_GROUP_DIM
POOL_HIST = max(POOL_WINDOWS) - 1
MEM_LEN = 256
MEM_HEADS = 4
MEM_HEAD_DIM = 128
MEM_WIDTH = MEM_HEADS * MEM_HEAD_DIM
OFF_P = 2 * A_WIDTH
OFF_Q = OFF_P + POOL_WIDTH
OFF_G = OFF_Q + MEM_WIDTH
C_IN = OFF_G + 3 * D_MODEL
N_EXPERTS = 32
TOP_K = 4
D_FF = D_MODEL
SWIGLU_LIMIT = 7.0
SWIGLU_ALPHA = 1.702
LN_EPS = 1e-5

V7X_VMEM_LIMIT_BYTES = 56 * 1024 * 1024

LANES = 128
ROW_TILES = D_MODEL // LANES

PROMPT_ROWS = 512
SAMPLE_SEQS = 8
ROUTER_ROWS = 512
MOVE_ROWS = 256
EXPERT_ROWS = 512
HIST_ROWS = 16


def _const_spec(shape):
    nd = len(shape)
    return pl.BlockSpec(shape, lambda *_: (0,) * nd, pipeline_mode=pl.Buffered(1))


def _layer_norm(x, g, b):
    mu = jnp.mean(x, axis=-1, keepdims=True)
    xc = x - mu
    var = jnp.mean(xc * xc, axis=-1, keepdims=True)
    return xc * lax.rsqrt(var + LN_EPS) * g + b


def _dot(a, b):
    return jnp.dot(a, b, preferred_element_type=F32)


def _softmax_rows(s):
    m = jnp.max(s, axis=-1, keepdims=True)
    e = jnp.exp(s - m)
    return e / jnp.sum(e, axis=-1, keepdims=True)


def _memproj_kernel(x_ref, w_ref, k_ref, v_ref):
    y = _dot(x_ref[...].astype(BF16), w_ref[...])
    k_ref[...] = y[:, :MEM_WIDTH]
    v_ref[...] = y[:, MEM_WIDTH:]


def _memproj(mem2d, w_kv):
    rows = mem2d.shape[0]
    blk = 512 if rows % 512 == 0 else MEM_LEN
    return pl.pallas_call(
        _memproj_kernel,
        grid=(rows // blk,),
        in_specs=[pl.BlockSpec((blk, D_MODEL), lambda i: (i, 0)),
                  _const_spec((D_MODEL, 2 * MEM_WIDTH))],
        out_specs=[pl.BlockSpec((blk, MEM_WIDTH), lambda i: (i, 0)),
                   pl.BlockSpec((blk, MEM_WIDTH), lambda i: (i, 0))],
        out_shape=[jax.ShapeDtypeStruct((rows, MEM_WIDTH), F32)] * 2,
        name="memproj",
    )(mem2d, w_kv)


def _project(xb, w_in_ref, b_in_ref, lo, hi):
    return _dot(xb, w_in_ref[:, lo:hi]) + b_in_ref[:, lo:hi]


def _branch_gates(xb, w_in_ref, b_in_ref):
    return [jax.nn.sigmoid(_project(xb, w_in_ref, b_in_ref, OFF_G + k * D_MODEL,
                                    OFF_G + (k + 1) * D_MODEL)) for k in range(3)]


def _merge_and_norm(x, gates, branches, wo_ref, g_ref, b_ref, alpha):
    h = gates[0] * branches[0] + gates[1] * branches[1] + gates[2] * branches[2]
    t = _dot(h.astype(BF16), wo_ref[...])
    return _layer_norm(alpha * x + t, g_ref[...], b_ref[...])


def _pool_branch(window_sum, tok, cnt, wpool_ref, ls_ref, wb_ref, rows):
    ys = []
    for g in range(len(POOL_WINDOWS)):
        d = window_sum(g) / cnt(g) - tok(g)
        ys.append(_dot(d.reshape(rows, POOL_GROUP_DIM).astype(BF16), wpool_ref[g]))
    y = jnp.concatenate(ys, axis=1) * ls_ref[...]
    return _dot(y.astype(BF16), wb_ref[...])


def _mix_prompt_kernel(x_ref, mk_ref, mv_ref, w_in_ref, b_in_ref, lnv_g_ref, lnv_b_ref, ws_ref,
                       bs_ref, wa_ref, wpool_ref, ls_ref, wb_ref, wm_ref, wo_ref, ln1_g_ref,
                       ln1_b_ref, y_ref, pool_ref, pbuf, *, alpha):
    j = pl.program_id(1)
    rows = x_ref.shape[0]
    x = x_ref[...]
    xb = x.astype(BF16)

    z = jax.nn.gelu(_project(xb, w_in_ref, b_in_ref, 0, OFF_P))
    u = z[:, :A_WIDTH]
    vn = _layer_norm(z[:, A_WIDTH:], lnv_g_ref[...], lnv_b_ref[...]).astype(BF16)
    tril = (lax.broadcasted_iota(jnp.int32, (CHUNK, CHUNK), 1)
            <= lax.broadcasted_iota(jnp.int32, (CHUNK, CHUNK), 0))
    w_tril = [jnp.where(tril, ws_ref[g], 0.0).astype(BF16) for g in range(A_GROUPS)]
    mixed = []
    for c in range(rows // CHUNK):
        vc = vn[c * CHUNK:(c + 1) * CHUNK]
        cols = [_dot(w_tril[g], vc[:, g * A_GROUP_DIM:(g + 1) * A_GROUP_DIM])
                for g in range(A_GROUPS)]
        mixed.append(jnp.concatenate(cols, axis=1) + bs_ref[...])
    mixed = jnp.concatenate(mixed, axis=0)
    br_a = _dot((u * mixed).astype(BF16), wa_ref[...])

    @pl.when(j == 0)
    def _():
        pbuf[0:HIST_ROWS, :] = jnp.zeros((HIST_ROWS, POOL_WIDTH), F32)

    pbuf[HIST_ROWS:HIST_ROWS + rows, :] = _project(xb, w_in_ref, b_in_ref, OFF_P, OFF_Q)
    pos = j * rows + lax.broadcasted_iota(jnp.int32, (rows, POOL_GROUP_DIM), 0)

    def window_sum(g):
        lo = g * POOL_GROUP_DIM
        acc = pbuf[HIST_ROWS:HIST_ROWS + rows, lo:lo + POOL_GROUP_DIM]
        for i in range(1, POOL_WINDOWS[g]):
            acc = acc + pbuf[HIST_ROWS - i:HIST_ROWS - i + rows, lo:lo + POOL_GROUP_DIM]
        return acc

    def tok(g):
        lo = g * POOL_GROUP_DIM
        return pbuf[HIST_ROWS:HIST_ROWS + rows, lo:lo + POOL_GROUP_DIM]

    def cnt(g):
        return jnp.minimum(POOL_WINDOWS[g], pos + 1).astype(F32)

    br_b = _pool_branch(window_sum, tok, cnt, wpool_ref, ls_ref, wb_ref, rows)
    pool_ref[...] = pbuf[rows + 1:rows + HIST_ROWS, :]
    pbuf[0:HIST_ROWS, :] = pbuf[rows:rows + HIST_ROWS, :]

    q = _project(xb, w_in_ref, b_in_ref, OFF_Q, OFF_G).astype(BF16)
    mk = mk_ref[...].astype(BF16)
    mv = mv_ref[...].astype(BF16)
    outs = []
    for h in range(MEM_HEADS):
        sl = slice(h * MEM_HEAD_DIM, (h + 1) * MEM_HEAD_DIM)
        s = lax.dot_general(q[:, sl], mk[:, sl], (((1,), (1,)), ((), ())),
                            preferred_element_type=F32) * (MEM_HEAD_DIM ** -0.5)
        outs.append(_dot(_softmax_rows(s).astype(BF16), mv[:, sl]))
    br_m = _dot(jnp.concatenate(outs, axis=1).astype(BF16), wm_ref[...])

    gates = _branch_gates(xb, w_in_ref, b_in_ref)
    y_ref[...] = _merge_and_norm(x, gates, (br_a, br_b, br_m), wo_ref, ln1_g_ref, ln1_b_ref,
                                 alpha)


def _mix_prompt(x2d, mk, mv, wts, *, batch, seq, alpha):
    rows = PROMPT_ROWS
    steps = seq // rows
    weight_specs = [_const_spec(w.shape) for w in wts]
    return pl.pallas_call(
        functools.partial(_mix_prompt_kernel, alpha=alpha),
        grid=(batch, steps),
        in_specs=[pl.BlockSpec((rows, D_MODEL), lambda n, j: (n * steps + j, 0)),
                  pl.BlockSpec((MEM_LEN, MEM_WIDTH), lambda n, j: (n, 0)),
                  pl.BlockSpec((MEM_LEN, MEM_WIDTH), lambda n, j: (n, 0))] + weight_specs,
        out_specs=[pl.BlockSpec((rows, D_MODEL), lambda n, j: (n * steps + j, 0)),
                   pl.BlockSpec((None, POOL_HIST, POOL_WIDTH), lambda n, j: (n, 0, 0))],
        out_shape=[jax.ShapeDtypeStruct((batch * seq, D_MODEL), F32),
                   jax.ShapeDtypeStruct((batch, POOL_HIST, POOL_WIDTH), F32)],
        scratch_shapes=[pltpu.VMEM((HIST_ROWS + rows, POOL_WIDTH), F32)],
        compiler_params=pltpu.CompilerParams(
            dimension_semantics=("arbitrary", "arbitrary"),
            vmem_limit_bytes=V7X_VMEM_LIMIT_BYTES),
        name="mix_prompt",
    )(x2d, mk, mv, *wts)


def _mix_sample_kernel(x_ref, hist_ref, kc_ref, vc_ref, w_in_ref, b_in_ref, lnv_g_ref, lnv_b_ref,
                       ws_ref, bs_ref, wa_ref, wpool_ref, ls_ref, wb_ref, wm_ref, wo_ref,
                       ln1_g_ref, ln1_b_ref, y_ref, pool_ref, vn_ref, pbuf, *, alpha, pos0, dec):
    rows = x_ref.shape[0]
    seqs = rows // dec
    x = x_ref[...]
    xb = x.astype(BF16)

    z = jax.nn.gelu(_project(xb, w_in_ref, b_in_ref, 0, OFF_P))
    u = z[:, :A_WIDTH]
    vn = _layer_norm(z[:, A_WIDTH:], lnv_g_ref[...], lnv_b_ref[...])
    vn_ref[...] = vn
    vnb = vn.astype(BF16)
    first = dec * lax.broadcasted_iota(jnp.int32, (seqs, dec, rows), 0)
    t = lax.broadcasted_iota(jnp.int32, (seqs, dec, rows), 1)
    c = lax.broadcasted_iota(jnp.int32, (seqs, dec, rows), 2)
    keep = (c >= first) & (c <= first + t)
    cols = []
    for g in range(A_GROUPS):
        wg = jnp.where(keep, ws_ref[g].reshape(seqs, dec, rows), 0.0)
        wg = wg.reshape(rows, rows).astype(BF16)
        cols.append(_dot(wg, vnb[:, g * A_GROUP_DIM:(g + 1) * A_GROUP_DIM]))
    mixed = jnp.concatenate(cols, axis=1) + bs_ref[...]
    br_a = _dot((u * mixed).astype(BF16), wa_ref[...])

    pbuf[:, 1:HIST_ROWS, :] = hist_ref[...]
    pbuf[:, HIST_ROWS:HIST_ROWS + dec, :] = _project(
        xb, w_in_ref, b_in_ref, OFF_P, OFF_Q).reshape(seqs, dec, POOL_WIDTH)
    pos = pos0 + lax.broadcasted_iota(jnp.int32, (seqs, dec, POOL_GROUP_DIM), 1)

    def window_sum(g):
        lo = g * POOL_GROUP_DIM
        acc = pbuf[:, HIST_ROWS:HIST_ROWS + dec, lo:lo + POOL_GROUP_DIM]
        for i in range(1, POOL_WINDOWS[g]):
            acc = acc + pbuf[:, HIST_ROWS - i:HIST_ROWS - i + dec, lo:lo + POOL_GROUP_DIM]
        return acc

    def tok(g):
        lo = g * POOL_GROUP_DIM
        return pbuf[:, HIST_ROWS:HIST_ROWS + dec, lo:lo + POOL_GROUP_DIM]

    def cnt(g):
        return jnp.minimum(POOL_WINDOWS[g], pos + 1).astype(F32)

    br_b = _pool_branch(window_sum, tok, cnt, wpool_ref, ls_ref, wb_ref, rows)
    pool_ref[...] = pbuf[:, dec + 1:dec + HIST_ROWS, :]

    q = _project(xb, w_in_ref, b_in_ref, OFF_Q, OFF_G).astype(BF16)
    outs = []
    for h in range(MEM_HEADS):
        sl = slice(h * MEM_HEAD_DIM, (h + 1) * MEM_HEAD_DIM)
        qh = q[:, sl].reshape(seqs, dec, MEM_HEAD_DIM)
        kh = kc_ref[:, pl.ds(h, MEM_LEN, stride=MEM_HEADS), :].astype(BF16)
        vh = vc_ref[:, pl.ds(h, MEM_LEN, stride=MEM_HEADS), :].astype(BF16)
        s = jnp.einsum("sld,smd->slm", qh, kh,
                       preferred_element_type=F32) * (MEM_HEAD_DIM ** -0.5)
        o = jnp.einsum("slm,smd->sld", _softmax_rows(s).astype(BF16), vh,
                       preferred_element_type=F32)
        outs.append(o.reshape(rows, MEM_HEAD_DIM))
    br_m = _dot(jnp.concatenate(outs, axis=1).astype(BF16), wm_ref[...])

    gates = _branch_gates(xb, w_in_ref, b_in_ref)
    y_ref[...] = _merge_and_norm(x, gates, (br_a, br_b, br_m), wo_ref, ln1_g_ref, ln1_b_ref,
                                 alpha)


def _mix_sample(x2d, hist, kc, vc, wts, *, layer, nseq, dec, pos0, alpha):
    seqs = SAMPLE_SEQS
    rows = seqs * dec
    steps = nseq // seqs
    weight_specs = [_const_spec(w.shape) for w in wts]
    return pl.pallas_call(
        functools.partial(_mix_sample_kernel, alpha=alpha, pos0=pos0, dec=dec),
        grid=(steps,),
        in_specs=[pl.BlockSpec((rows, D_MODEL), lambda i: (i, 0)),
                  pl.BlockSpec((seqs, POOL_HIST, POOL_WIDTH), lambda i: (layer * steps + i, 0, 0)),
                  pl.BlockSpec((seqs, MEM_LEN * MEM_HEADS, MEM_HEAD_DIM),
                               lambda i: (layer * steps + i, 0, 0)),
                  pl.BlockSpec((seqs, MEM_LEN * MEM_HEADS, MEM_HEAD_DIM),
                               lambda i: (layer * steps + i, 0, 0))]
                 + weight_specs,
        out_specs=[pl.BlockSpec((rows, D_MODEL), lambda i: (i, 0)),
                   pl.BlockSpec((seqs, POOL_HIST, POOL_WIDTH), lambda i: (i, 0, 0)),
                   pl.BlockSpec((rows, A_WIDTH), lambda i: (i, 0))],
        out_shape=[jax.ShapeDtypeStruct((nseq * dec, D_MODEL), F32),
                   jax.ShapeDtypeStruct((nseq, POOL_HIST, POOL_WIDTH), F32),
                   jax.ShapeDtypeStruct((nseq * dec, A_WIDTH), F32)],
        scratch_shapes=[pltpu.VMEM((seqs, HIST_ROWS + dec, POOL_WIDTH), F32)],
        compiler_params=pltpu.CompilerParams(
            dimension_semantics=("arbitrary",),
            vmem_limit_bytes=V7X_VMEM_LIMIT_BYTES),
        name="mix_sample",
    )(x2d, hist, kc, vc, *wts)


def _split_bf16(a):
    hi = a.astype(BF16)
    lo = (a - hi.astype(F32)).astype(BF16)
    return hi, lo


def _router_kernel(yp_ref, ys_ref, wr_ref, br_ref, e_ref, g_ref, rank_ref, cnt_ref, carry, earlier,
                   *, prompt_tiles):
    i = pl.program_id(0)
    rows = yp_ref.shape[0]

    @pl.when(i == 0)
    def _():
        carry[...] = jnp.zeros(carry.shape, F32)
        before = (lax.broadcasted_iota(jnp.int32, (rows, rows), 0)
                  < lax.broadcasted_iota(jnp.int32, (rows, rows), 1))
        earlier[...] = jnp.where(before, 1.0, 0.0).astype(BF16)

    x = jnp.where(i < prompt_tiles, yp_ref[...], ys_ref[...])
    x_hi, x_lo = _split_bf16(x)
    w_hi, w_lo = _split_bf16(wr_ref[...])
    nt = (((1,), (1,)), ((), ()))
    logits = (lax.dot_general(w_hi, x_hi, nt, preferred_element_type=F32)
              + lax.dot_general(w_hi, x_lo, nt, preferred_element_type=F32)
              + lax.dot_general(w_lo, x_hi, nt, preferred_element_type=F32)) + br_ref[...]

    eid = lax.broadcasted_iota(jnp.int32, (N_EXPERTS, rows), 0)
    work = logits
    vals, idxs, sels = [], [], []
    for _ in range(TOP_K):
        m = jnp.max(work, axis=0, keepdims=True)
        idx = jnp.min(jnp.where(work == m, eid, N_EXPERTS), axis=0, keepdims=True)
        sel = eid == idx
        work = jnp.where(sel, -jnp.inf, work)
        vals.append(m)
        idxs.append(idx)
        sels.append(sel)
    top_v = jnp.concatenate(vals, axis=0)
    ex = jnp.exp(top_v - top_v[0:1])
    g_ref[...] = ex / jnp.sum(ex, axis=0, keepdims=True)
    e_ref[...] = jnp.concatenate(idxs, axis=0)

    onehot = jnp.zeros((N_EXPERTS, rows), F32)
    for sel in sels:
        onehot = onehot + jnp.where(sel, 1.0, 0.0)
    prefix = _dot(onehot.astype(BF16), earlier[...]) + carry[...]
    ranks = [jnp.sum(jnp.where(sel, prefix, 0.0), axis=0, keepdims=True) for sel in sels]
    rank_ref[...] = jnp.concatenate(ranks, axis=0).astype(jnp.int32)
    total = carry[...] + jnp.sum(onehot, axis=1, keepdims=True)
    carry[...] = total
    cnt_ref[...] = jnp.broadcast_to(total, cnt_ref.shape).astype(jnp.int32)


def _router(y_p, y_s, w_r_t, b_r_col):
    rows = ROUTER_ROWS
    tp, ts = y_p.shape[0], y_s.shape[0]
    pt, st = tp // rows, ts // rows
    total = tp + ts
    tok_spec = pl.BlockSpec((TOP_K, rows), lambda i: (0, i))
    return pl.pallas_call(
        functools.partial(_router_kernel, prompt_tiles=pt),
        grid=(pt + st,),
        in_specs=[pl.BlockSpec((rows, D_MODEL), lambda i: (jnp.minimum(i, pt - 1), 0)),
                  pl.BlockSpec((rows, D_MODEL), lambda i: (jnp.maximum(i - pt, 0), 0)),
                  _const_spec((N_EXPERTS, D_MODEL)),
                  _const_spec((N_EXPERTS, 1))],
        out_specs=[tok_spec, tok_spec, tok_spec,
                   pl.BlockSpec((N_EXPERTS, 128), lambda i: (0, 0))],
        out_shape=[jax.ShapeDtypeStruct((TOP_K, total), jnp.int32),
                   jax.ShapeDtypeStruct((TOP_K, total), F32),
                   jax.ShapeDtypeStruct((TOP_K, total), jnp.int32),
                   jax.ShapeDtypeStruct((N_EXPERTS, 128), jnp.int32)],
        scratch_shapes=[pltpu.VMEM((N_EXPERTS, 1), F32), pltpu.VMEM((rows, rows), BF16)],
        compiler_params=pltpu.CompilerParams(dimension_semantics=("arbitrary",)),
        name="router",
    )(y_p, y_s, w_r_t, b_r_col)


def _to_row_tiles(dst, val):
    rows = val.shape[0]
    for c in range(ROW_TILES):
        dst[pl.ds(c, rows, stride=ROW_TILES), :] = val[:, c * LANES:(c + 1) * LANES]


def _from_row_tiles(src, start, rows):
    return jnp.concatenate(
        [src[pl.ds(start + c, rows, stride=ROW_TILES), :] for c in range(ROW_TILES)], axis=1)


def _start_alternating(copy, k):
    copy.start(priority=k % 2)


def _dispatch_kernel(dest_ref, fill_lo_ref, fill_hi_ref, yp_ref, ys_ref, xs_ref, src, zbuf, sem,
                     *, prompt_tiles, total):
    i = pl.program_id(0)
    rows = yp_ref.shape[0]
    slot = lax.rem(i, 2)

    @pl.when(i == 0)
    def _():
        zbuf[...] = jnp.zeros(zbuf.shape, F32)

        def zero_block(b):
            at = pl.multiple_of(b * EXPERT_ROWS, EXPERT_ROWS)
            return pltpu.make_async_copy(zbuf, xs_ref.at[pl.ds(at, EXPERT_ROWS)], sem.at[2])

        def zero_padding(e, act):
            at, left = fill_lo_ref[e], fill_hi_ref[e] - fill_lo_ref[e]
            size = EXPERT_ROWS // 2
            while size:
                has = (left & size) != 0

                @pl.when(has)
                def _(at=at, size=size):
                    act(pltpu.make_async_copy(zbuf.at[pl.ds(0, size)],
                                              xs_ref.at[pl.ds(at, size)], sem.at[2]))

                at = at + jnp.where(has, size, 0)
                size //= 2

        def each_expert(act):
            lax.fori_loop(0, N_EXPERTS, lambda e, c: (zero_padding(e, act), c)[1], 0)

        lo, hi = fill_hi_ref[N_EXPERTS - 1] // EXPERT_ROWS, xs_ref.shape[0] // EXPERT_ROWS
        each_expert(lambda c: c.start())
        lax.fori_loop(lo, hi, lambda b, c: (zero_block(b).start(), c)[1], 0)
        each_expert(lambda c: c.wait())
        lax.fori_loop(lo, hi, lambda b, c: (zero_block(b).wait(), c)[1], 0)

    def row_copy(s, r, d):
        at = r * ROW_TILES
        if not isinstance(at, int):
            at = pl.multiple_of(at, ROW_TILES)
        return pltpu.make_async_copy(src.at[s, pl.ds(at, ROW_TILES)], xs_ref.at[d], sem.at[s])

    def copies(step, s, act):
        def body(r, carry):
            for k in range(TOP_K):
                act(row_copy(s, r, dest_ref[k * total + step * rows + r]), k)
            return carry
        lax.fori_loop(0, rows, body, 0, unroll=4)

    _to_row_tiles(src.at[slot], jnp.where(i < prompt_tiles, yp_ref[...], ys_ref[...]))
    for r in range(rows):
        for k in range(TOP_K):
            _start_alternating(row_copy(slot, r, dest_ref[k * total + i * rows + r]), k)

    @pl.when(i > 0)
    def _():
        for r in range(rows):
            for k in range(TOP_K):
                row_copy(1 - slot, r, dest_ref[k * total + (i - 1) * rows + r]).wait()

    @pl.when(i == pl.num_programs(0) - 1)
    def _():
        copies(i, slot, lambda c, k: c.wait())


def _dispatch(dest_flat, fill_lo, fill_hi, y_p, y_s, n_slots):
    rows = MOVE_ROWS
    tp, ts = y_p.shape[0], y_s.shape[0]
    pt, st = tp // rows, ts // rows
    return pl.pallas_call(
        functools.partial(_dispatch_kernel, prompt_tiles=pt, total=tp + ts),
        grid_spec=pltpu.PrefetchScalarGridSpec(
            num_scalar_prefetch=3,
            grid=(pt + st,),
            in_specs=[pl.BlockSpec((rows, D_MODEL), lambda i, *_: (jnp.minimum(i, pt - 1), 0)),
                      pl.BlockSpec((rows, D_MODEL), lambda i, *_: (jnp.maximum(i - pt, 0), 0))],
            out_specs=pl.BlockSpec(memory_space=pl.ANY),
            scratch_shapes=[pltpu.VMEM((2, rows * ROW_TILES, LANES), F32),
                            pltpu.VMEM((EXPERT_ROWS, ROW_TILES, LANES), F32),
                            pltpu.SemaphoreType.DMA((3,))]),
        out_shape=jax.ShapeDtypeStruct((n_slots, ROW_TILES, LANES), F32),
        compiler_params=pltpu.CompilerParams(dimension_semantics=("arbitrary",)),
        name="dispatch",
    )(dest_flat, fill_lo, fill_hi, y_p, y_s)


def _experts_kernel(blk_e_ref, first_ref, buf_ref, next_ref, nact_ref, x_ref, w1_hbm, b1_ref,
                    w2_hbm, b2_ref, o_ref, wf1, wf2, sems):
    b = pl.program_id(0)
    active = b < nact_ref[0]
    rows = x_ref.shape[0] // ROW_TILES

    def fetch(e, half):
        return (pltpu.make_async_copy(w1_hbm.at[e], wf1.at[half], sems.at[0, half]),
                pltpu.make_async_copy(w2_hbm.at[e], wf2.at[half], sems.at[1, half]))

    @pl.when(b == 0)
    def _():
        for c in fetch(blk_e_ref[0], buf_ref[0]):
            c.start()

    @pl.when(jnp.logical_not(active))
    def _():
        o_ref[...] = jnp.zeros(o_ref.shape, F32)

    @pl.when(active & (first_ref[b] == 1))
    def _():
        half = buf_ref[b]
        for c in fetch(blk_e_ref[b], half):
            c.wait()

        @pl.when(next_ref[b] >= 0)
        def _():
            for c in fetch(next_ref[b], 1 - half):
                c.start(priority=1)

    @pl.when(active)
    def _():
        half = buf_ref[b]
        x = _from_row_tiles(x_ref, 0, rows).astype(BF16)
        h = _dot(x, wf1[half].astype(BF16)) + b1_ref[...]
        g = jnp.minimum(h[:, :D_FF], SWIGLU_LIMIT)
        lin = jnp.clip(h[:, D_FF:], -SWIGLU_LIMIT, SWIGLU_LIMIT)
        act = g * jax.nn.sigmoid(SWIGLU_ALPHA * g) * (lin + 1.0)
        _to_row_tiles(o_ref, _dot(act.astype(BF16), wf2[half].astype(BF16)) + b2_ref[...])


def _experts(blk_e, blk_first, blk_buf, blk_next, nact, xs, w1, b1, w2, b2):
    rows = EXPERT_ROWS
    n_slots = xs.shape[0]

    def slot_map(b, blk_e, first, buf, nxt, nact):
        return (jnp.minimum(b, nact[0] - 1), 0)

    def expert_map(b, blk_e, first, buf, nxt, nact):
        return (blk_e[b], 0, 0)

    out = pl.pallas_call(
        _experts_kernel,
        grid_spec=pltpu.PrefetchScalarGridSpec(
            num_scalar_prefetch=5,
            grid=(n_slots // rows,),
            in_specs=[pl.BlockSpec((rows * ROW_TILES, LANES), slot_map),
                      pl.BlockSpec(memory_space=pl.ANY),
                      pl.BlockSpec((None, 1, 2 * D_FF), expert_map),
                      pl.BlockSpec(memory_space=pl.ANY),
                      pl.BlockSpec((None, 1, D_MODEL), expert_map)],
            out_specs=pl.BlockSpec((rows * ROW_TILES, LANES), lambda b, *_: (b, 0)),
            scratch_shapes=[pltpu.VMEM((2, D_MODEL, 2 * D_FF), F32),
                            pltpu.VMEM((2, D_FF, D_MODEL), F32),
                            pltpu.SemaphoreType.DMA((2, 2))]),
        out_shape=jax.ShapeDtypeStruct((n_slots * ROW_TILES, LANES), F32),
        compiler_params=pltpu.CompilerParams(
            dimension_semantics=("arbitrary",),
            vmem_limit_bytes=V7X_VMEM_LIMIT_BYTES),
        name="experts",
    )(blk_e, blk_first, blk_buf, blk_next, nact, xs.reshape(n_slots * ROW_TILES, LANES),
      w1, b1, w2, b2)
    return out.reshape(n_slots, ROW_TILES, LANES)


def _combine_kernel(dest_ref, y_ref, g_ref, os_ref, ln_g_ref, ln_b_ref, o_ref, buf_a, buf_b, sem,
                    *, first, total, alpha):
    i = pl.program_id(0)
    last = pl.num_programs(0) - 1
    rows = y_ref.shape[0]

    def row_copy(buf, s, k, r, d):
        at = (k * rows + r) * ROW_TILES
        if not isinstance(at, int):
            at = pl.multiple_of(at, ROW_TILES)
        return pltpu.make_async_copy(os_ref.at[d], buf.at[pl.ds(at, ROW_TILES)], sem.at[s])

    def slot_of(step, k, r):
        return dest_ref[k * total + first + step * rows + r]

    def looped(buf, s, step, act):
        def body(r, carry):
            for k in range(TOP_K):
                act(row_copy(buf, s, k, r, slot_of(step, k, r)), k)
            return carry
        lax.fori_loop(0, rows, body, 0, unroll=4)

    def start_unrolled(buf, s, step):
        for r in range(rows):
            for k in range(TOP_K):
                _start_alternating(row_copy(buf, s, k, r, slot_of(step, k, r)), k)

    def finish(buf):
        g = g_ref[...]
        f = g[:, 0:1] * _from_row_tiles(buf, 0, rows)
        for k in range(1, TOP_K):
            f = f + g[:, k:k + 1] * _from_row_tiles(buf, k * rows * ROW_TILES, rows)
        o_ref[...] = _layer_norm(alpha * y_ref[...] + f, ln_g_ref[...], ln_b_ref[...])

    @pl.when(i == 0)
    def _():
        looped(buf_a, 0, 0, _start_alternating)

    ahead = jnp.minimum(i + 1, last)
    for parity, (cur, nxt) in enumerate(((buf_a, buf_b), (buf_b, buf_a))):
        @pl.when(lax.rem(i, 2) == parity)
        def _():
            for r in range(rows):
                for k in range(TOP_K):
                    row_copy(cur, parity, k, r, slot_of(i, k, r)).wait()
            start_unrolled(nxt, 1 - parity, ahead)
            finish(cur)

            @pl.when(i == last)
            def _():
                looped(nxt, 1 - parity, ahead, lambda c, k: c.wait())


def _combine(dest_flat, y, gates, out_sorted, ln_g, ln_b, *, first, total, alpha):
    rows = MOVE_ROWS
    off = first // rows
    return pl.pallas_call(
        functools.partial(_combine_kernel, first=first, total=total, alpha=alpha),
        grid_spec=pltpu.PrefetchScalarGridSpec(
            num_scalar_prefetch=1,
            grid=(y.shape[0] // rows,),
            in_specs=[pl.BlockSpec((rows, D_MODEL), lambda i, d: (i, 0)),
                      pl.BlockSpec((rows, TOP_K), lambda i, d: (off + i, 0)),
                      pl.BlockSpec(memory_space=pl.ANY),
                      pl.BlockSpec((1, D_MODEL), lambda i, d: (0, 0)),
                      pl.BlockSpec((1, D_MODEL), lambda i, d: (0, 0))],
            out_specs=pl.BlockSpec((rows, D_MODEL), lambda i, d: (i, 0)),
            scratch_shapes=[pltpu.VMEM((TOP_K * rows * ROW_TILES, LANES), F32),
                            pltpu.VMEM((TOP_K * rows * ROW_TILES, LANES), F32),
                            pltpu.SemaphoreType.DMA((2,))]),
        out_shape=jax.ShapeDtypeStruct(y.shape, F32),
        compiler_params=pltpu.CompilerParams(dimension_semantics=("arbitrary",)),
        name="combine",
    )(dest_flat, y, gates, out_sorted, ln_g, ln_b)


def _moe(y_p, y_s, w_r, b_r, w1, b1, w2, b2, ln_g, ln_b, *, layer, alpha):
    total = y_p.shape[0] + y_s.shape[0]
    top_e, gates, rank, counts = _router(y_p, y_s, w_r.T, b_r.reshape(N_EXPERTS, 1))

    blk = EXPERT_ROWS
    n_blocks = -(-(total * TOP_K) // blk) + N_EXPERTS
    counts = counts[:, 0]
    padded = (counts + blk - 1) // blk * blk
    pad_end = jnp.cumsum(padded)
    pad_start = pad_end - padded
    experts = jnp.arange(N_EXPERTS, dtype=jnp.int32)[:, None, None]
    start_of = jnp.sum(jnp.where(top_e[None] == experts, pad_start[:, None, None], 0), axis=0)
    dest = (start_of + rank).reshape(TOP_K * total)
    nact = pad_end[-1] // blk
    blk_ids = jnp.minimum(jnp.arange(n_blocks, dtype=jnp.int32), nact - 1)
    blk_e = jnp.sum((pad_end[None, :] <= (blk_ids * blk)[:, None]).astype(jnp.int32), axis=1)
    blk_first = jnp.concatenate([jnp.ones((1,), jnp.int32),
                                 (blk_e[1:] != blk_e[:-1]).astype(jnp.int32)])
    blk_buf = (jnp.cumsum(blk_first) - 1) % 2
    ids = jnp.arange(N_EXPERTS, dtype=jnp.int32)
    later_used = (ids[None, :] > ids[:, None]) & (padded[None, :] > 0)
    next_e = jnp.min(jnp.where(later_used, ids[None, :], N_EXPERTS), axis=1)
    next_e = jnp.where(next_e < N_EXPERTS, next_e + layer * N_EXPERTS, -1)
    blk_next = jnp.sum(jnp.where(blk_e[:, None] == ids[None, :], next_e[None, :], 0), axis=1)

    xs = _dispatch(dest, pad_start + counts, pad_end, y_p, y_s, n_blocks * blk)
    out_sorted = _experts(blk_e + layer * N_EXPERTS, blk_first, blk_buf, blk_next, nact.reshape(1),
                          xs, w1.reshape(-1, D_MODEL, 2 * D_FF), b1.reshape(-1, 1, 2 * D_FF),
                          w2.reshape(-1, D_FF, D_MODEL), b2.reshape(-1, 1, D_MODEL))
    gates = gates.T
    ln_g, ln_b = ln_g.reshape(1, D_MODEL), ln_b.reshape(1, D_MODEL)
    return (_combine(dest, y_p, gates, out_sorted, ln_g, ln_b, first=0, total=total, alpha=alpha),
            _combine(dest, y_s, gates, out_sorted, ln_g, ln_b, first=y_p.shape[0], total=total,
                     alpha=alpha))


def kernel(x_prompt, x_sample, state_pool, cache_mem_k, cache_mem_v, mem_prompt, w_in, b_in, ln_v_g, ln_v_b, w_s, b_s, w_a_out, w_pool, ls_pool, w_b_out, w_mk, w_mv, w_m_out, w_o, ln1_g, ln1_b, w_r, b_r, w1, b1, w2, b2, ln2_g, ln2_b):
    depth = w_in.shape[0]
    batch, seq, _ = x_prompt.shape
    nseq, dec, _ = x_sample.shape
    pos0 = 16384
    alpha = (2 * depth) ** 0.25
    assert seq % PROMPT_ROWS == 0 and nseq % SAMPLE_SEQS == 0 and dec <= CHUNK and pos0 % CHUNK == 0
    assert (batch * seq) % ROUTER_ROWS == 0 and (nseq * dec) % ROUTER_ROWS == 0

    y_p = x_prompt.reshape(batch * seq, D_MODEL)
    y_s = x_sample.reshape(nseq * dec, D_MODEL)
    pool_p_out, pool_s_out, mk_out, mv_out, cv_out = [], [], [], [], []
    for l in range(depth):
        mk_p, mv_p = _memproj(mem_prompt.reshape(batch * MEM_LEN, D_MODEL),
                              jnp.concatenate([w_mk[l], w_mv[l]], axis=1).astype(BF16))
        bias = jnp.repeat(b_s[l].T, A_GROUP_DIM, axis=1)
        shared = (w_in[l].astype(BF16), b_in[l].reshape(1, C_IN), ln_v_g[l].reshape(1, A_WIDTH),
                  ln_v_b[l].reshape(1, A_WIDTH))
        tail = (w_a_out[l].astype(BF16), w_pool[l].astype(BF16), ls_pool[l].reshape(1, POOL_WIDTH),
                w_b_out[l].astype(BF16), w_m_out[l].astype(BF16), w_o[l].astype(BF16),
                ln1_g[l].reshape(1, D_MODEL), ln1_b[l].reshape(1, D_MODEL))
        y1_p, pool_p = _mix_prompt(y_p, mk_p, mv_p, shared + (w_s[l], bias) + tail,
                                   batch=batch, seq=seq, alpha=alpha)
        reps = SAMPLE_SEQS
        y1_s, pool_s, v_s = _mix_sample(
            y_s, state_pool.reshape(depth * nseq, POOL_HIST, POOL_WIDTH),
            cache_mem_k.reshape(depth * nseq, MEM_LEN * MEM_HEADS, MEM_HEAD_DIM),
            cache_mem_v.reshape(depth * nseq, MEM_LEN * MEM_HEADS, MEM_HEAD_DIM),
            shared + (jnp.tile(w_s[l][:, :dec, :dec], (1, reps, reps)),
                      jnp.tile(bias[:dec], (reps, 1))) + tail,
            layer=l, nseq=nseq, dec=dec, pos0=pos0, alpha=alpha)
        y_p, y_s = _moe(y1_p, y1_s, w_r[l], b_r[l], w1, b1, w2, b2, ln2_g[l], ln2_b[l],
                        layer=l, alpha=alpha)
        pool_p_out.append(pool_p)
        pool_s_out.append(pool_s)
        mk_out.append(mk_p.reshape(batch, MEM_LEN, MEM_HEADS, MEM_HEAD_DIM))
        mv_out.append(mv_p.reshape(batch, MEM_LEN, MEM_HEADS, MEM_HEAD_DIM))
        cv_out.append(v_s.reshape(nseq, dec, A_WIDTH))
    return (y_p.reshape(batch, seq, D_MODEL), y_s.reshape(nseq, dec, D_MODEL),
            jnp.stack(pool_p_out), jnp.stack(pool_s_out), jnp.stack(mk_out), jnp.stack(mv_out),
            jnp.stack(cv_out))
```

```python
import functools

import jax
import jax.numpy as jnp
from jax import lax
from jax.experimental import pallas as pl
from jax.experimental.pallas import tpu as pltpu

F32 = jnp.float32
BF16 = jnp.bfloat16

D_MODEL = 1024
CHUNK = 128
A_GROUPS = 8
A_GROUP_DIM = 128
A_WIDTH = A_GROUPS * A_GROUP_DIM
POOL_WINDOWS = (2, 4, 8, 16)
POOL_GROUP_DIM = 128
POOL_WIDTH = len(POOL_WINDOWS) * POOL_GROUP_DIM
POOL_HIST = max(POOL_WINDOWS) - 1
MEM_LEN = 256
MEM_HEADS = 4
MEM_HEAD_DIM = 128
MEM_WIDTH = MEM_HEADS * MEM_HEAD_DIM
OFF_P = 2 * A_WIDTH
OFF_Q = OFF_P + POOL_WIDTH
OFF_G = OFF_Q + MEM_WIDTH
C_IN = OFF_G + 3 * D_MODEL
N_EXPERTS = 32
TOP_K = 4
D_FF = D_MODEL
SWIGLU_LIMIT = 7.0
SWIGLU_ALPHA = 1.702
LN_EPS = 1e-5

V7X_VMEM_LIMIT_BYTES = 56 * 1024 * 1024

LANES = 128
ROW_TILES = D_MODEL // LANES

PROMPT_ROWS = 512
SAMPLE_SEQS = 8
ROUTER_ROWS = 512
MOVE_ROWS = 256
EXPERT_ROWS = 512
HIST_ROWS = 16


def _const_spec(shape):
    nd = len(shape)
    return pl.BlockSpec(shape, lambda *_: (0,) * nd, pipeline_mode=pl.Buffered(1))


def _layer_norm(x, g, b):
    mu = jnp.mean(x, axis=-1, keepdims=True)
    xc = x - mu
    var = jnp.mean(xc * xc, axis=-1, keepdims=True)
    return xc * lax.rsqrt(var + LN_EPS) * g + b


def _dot(a, b):
    return jnp.dot(a, b, preferred_element_type=F32)


def _softmax_rows(s):
    m = jnp.max(s, axis=-1, keepdims=True)
    e = jnp.exp(s - m)
    return e / jnp.sum(e, axis=-1, keepdims=True)


def _project(xb, w_in_ref, b_in_ref, lo, hi):
    return _dot(xb, w_in_ref[:, lo:hi]) + b_in_ref[:, lo:hi]


def _branch_gates(xb, w_in_ref, b_in_ref):
    return [jax.nn.sigmoid(_project(xb, w_in_ref, b_in_ref, OFF_G + k * D_MODEL,
                                    OFF_G + (k + 1) * D_MODEL)) for k in range(3)]


def _merge_and_norm(x, gates, branches, wo_ref, g_ref, b_ref, alpha):
    h = gates[0] * branches[0] + gates[1] * branches[1] + gates[2] * branches[2]
    t = _dot(h.astype(BF16), wo_ref[...])
    return _layer_norm(alpha * x + t, g_ref[...], b_ref[...])


def _pool_branch(window_sum, tok, cnt, wpool_ref, ls_ref, wb_ref, rows):
    ys = []
    for g in range(len(POOL_WINDOWS)):
        d = window_sum(g) / cnt(g) - tok(g)
        ys.append(_dot(d.reshape(rows, POOL_GROUP_DIM).astype(BF16), wpool_ref[g]))
    y = jnp.concatenate(ys, axis=1) * ls_ref[...]
    return _dot(y.astype(BF16), wb_ref[...])


def _mix_prompt_kernel(x_ref, mem_ref, wkv_ref, w_in_ref, b_in_ref, lnv_g_ref, lnv_b_ref, ws_ref,
                       bs_ref, wa_ref, wpool_ref, ls_ref, wb_ref, wm_ref, wo_ref, ln1_g_ref,
                       ln1_b_ref, y_ref, pool_ref, mk_ref, mv_ref, pbuf, mkb, mvb, *, alpha):
    j = pl.program_id(1)
    rows = x_ref.shape[0]
    x = x_ref[...]
    xb = x.astype(BF16)

    @pl.when(j == 0)
    def _():
        kv = _dot(mem_ref[...].astype(BF16), wkv_ref[...])
        mk_ref[...] = kv[:, :MEM_WIDTH]
        mv_ref[...] = kv[:, MEM_WIDTH:]
        mkb[...] = kv[:, :MEM_WIDTH].astype(BF16)
        mvb[...] = kv[:, MEM_WIDTH:].astype(BF16)

    z = jax.nn.gelu(_project(xb, w_in_ref, b_in_ref, 0, OFF_P))
    u = z[:, :A_WIDTH]
    vn = _layer_norm(z[:, A_WIDTH:], lnv_g_ref[...], lnv_b_ref[...]).astype(BF16)
    tril = (lax.broadcasted_iota(jnp.int32, (CHUNK, CHUNK), 1)
            <= lax.broadcasted_iota(jnp.int32, (CHUNK, CHUNK), 0))
    w_tril = [jnp.where(tril, ws_ref[g], 0.0).astype(BF16) for g in range(A_GROUPS)]
    mixed = []
    for c in range(rows // CHUNK):
        vc = vn[c * CHUNK:(c + 1) * CHUNK]
        cols = [_dot(w_tril[g], vc[:, g * A_GROUP_DIM:(g + 1) * A_GROUP_DIM])
                for g in range(A_GROUPS)]
        mixed.append(jnp.concatenate(cols, axis=1) + bs_ref[...])
    mixed = jnp.concatenate(mixed, axis=0)
    br_a = _dot((u * mixed).astype(BF16), wa_ref[...])

    @pl.when(j == 0)
    def _():
        pbuf[0:HIST_ROWS, :] = jnp.zeros((HIST_ROWS, POOL_WIDTH), F32)

    pbuf[HIST_ROWS:HIST_ROWS + rows, :] = _project(xb, w_in_ref, b_in_ref, OFF_P, OFF_Q)
    pos = j * rows + lax.broadcasted_iota(jnp.int32, (rows, POOL_GROUP_DIM), 0)

    def window_sum(g):
        lo = g * POOL_GROUP_DIM
        acc = pbuf[HIST_ROWS:HIST_ROWS + rows, lo:lo + POOL_GROUP_DIM]
        for i in range(1, POOL_WINDOWS[g]):
            acc = acc + pbuf[HIST_ROWS - i:HIST_ROWS - i + rows, lo:lo + POOL_GROUP_DIM]
        return acc

    def tok(g):
        lo = g * POOL_GROUP_DIM
        return pbuf[HIST_ROWS:HIST_ROWS + rows, lo:lo + POOL_GROUP_DIM]

    def cnt(g):
        return jnp.minimum(POOL_WINDOWS[g], pos + 1).astype(F32)

    br_b = _pool_branch(window_sum, tok, cnt, wpool_ref, ls_ref, wb_ref, rows)
    pool_ref[...] = pbuf[rows + 1:rows + HIST_ROWS, :]
    pbuf[0:HIST_ROWS, :] = pbuf[rows:rows + HIST_ROWS, :]

    q = _project(xb, w_in_ref, b_in_ref, OFF_Q, OFF_G).astype(BF16)
    mk = mkb[...]
    mv = mvb[...]
    outs = []
    for h in range(MEM_HEADS):
        sl = slice(h * MEM_HEAD_DIM, (h + 1) * MEM_HEAD_DIM)
        s = lax.dot_general(q[:, sl], mk[:, sl], (((1,), (1,)), ((), ())),
                            preferred_element_type=F32) * (MEM_HEAD_DIM ** -0.5)
        outs.append(_dot(_softmax_rows(s).astype(BF16), mv[:, sl]))
    br_m = _dot(jnp.concatenate(outs, axis=1).astype(BF16), wm_ref[...])

    gates = _branch_gates(xb, w_in_ref, b_in_ref)
    y_ref[...] = _merge_and_norm(x, gates, (br_a, br_b, br_m), wo_ref, ln1_g_ref, ln1_b_ref,
                                 alpha)


def _mix_prompt(x2d, mem2d, w_kv, wts, *, batch, seq, alpha):
    rows = PROMPT_ROWS
    steps = seq // rows
    weight_specs = [_const_spec(w.shape) for w in (w_kv,) + wts]
    mem_spec = pl.BlockSpec((MEM_LEN, MEM_WIDTH), lambda n, j: (n, 0))
    return pl.pallas_call(
        functools.partial(_mix_prompt_kernel, alpha=alpha),
        grid=(batch, steps),
        in_specs=[pl.BlockSpec((rows, D_MODEL), lambda n, j: (n * steps + j, 0)),
                  pl.BlockSpec((MEM_LEN, D_MODEL), lambda n, j: (n, 0))] + weight_specs,
        out_specs=[pl.BlockSpec((rows, D_MODEL), lambda n, j: (n * steps + j, 0)),
                   pl.BlockSpec((None, POOL_HIST, POOL_WIDTH), lambda n, j: (n, 0, 0)),
                   mem_spec, mem_spec],
        out_shape=[jax.ShapeDtypeStruct((batch * seq, D_MODEL), F32),
                   jax.ShapeDtypeStruct((batch, POOL_HIST, POOL_WIDTH), F32),
                   jax.ShapeDtypeStruct((batch * MEM_LEN, MEM_WIDTH), F32),
                   jax.ShapeDtypeStruct((batch * MEM_LEN, MEM_WIDTH), F32)],
        scratch_shapes=[pltpu.VMEM((HIST_ROWS + rows, POOL_WIDTH), F32),
                        pltpu.VMEM((MEM_LEN, MEM_WIDTH), BF16),
                        pltpu.VMEM((MEM_LEN, MEM_WIDTH), BF16)],
        compiler_params=pltpu.CompilerParams(
            dimension_semantics=("arbitrary", "arbitrary"),
            vmem_limit_bytes=V7X_VMEM_LIMIT_BYTES),
        name="mix_prompt",
    )(x2d, mem2d, w_kv, *wts)


def _mix_sample_kernel(x_ref, hist_ref, kc_ref, vc_ref, w_in_ref, b_in_ref, lnv_g_ref, lnv_b_ref,
                       ws_ref, bs_ref, wa_ref, wpool_ref, ls_ref, wb_ref, wm_ref, wo_ref,
                       ln1_g_ref, ln1_b_ref, y_ref, pool_ref, vn_ref, pbuf, *, alpha, pos0, dec):
    rows = x_ref.shape[0]
    seqs = rows // dec
    x = x_ref[...]
    xb = x.astype(BF16)

    z = jax.nn.gelu(_project(xb, w_in_ref, b_in_ref, 0, OFF_P))
    u = z[:, :A_WIDTH]
    vn = _layer_norm(z[:, A_WIDTH:], lnv_g_ref[...], lnv_b_ref[...])
    vn_ref[...] = vn
    vnb = vn.astype(BF16)
    first = dec * lax.broadcasted_iota(jnp.int32, (seqs, dec, rows), 0)
    t = lax.broadcasted_iota(jnp.int32, (seqs, dec, rows), 1)
    c = lax.broadcasted_iota(jnp.int32, (seqs, dec, rows), 2)
    keep = (c >= first) & (c <= first + t)
    cols = []
    for g in range(A_GROUPS):
        wg = jnp.where(keep, ws_ref[g].reshape(seqs, dec, rows), 0.0)
        wg = wg.reshape(rows, rows).astype(BF16)
        cols.append(_dot(wg, vnb[:, g * A_GROUP_DIM:(g + 1) * A_GROUP_DIM]))
    mixed = jnp.concatenate(cols, axis=1) + bs_ref[...]
    br_a = _dot((u * mixed).astype(BF16), wa_ref[...])

    pbuf[:, 1:HIST_ROWS, :] = hist_ref[...]
    pbuf[:, HIST_ROWS:HIST_ROWS + dec, :] = _project(
        xb, w_in_ref, b_in_ref, OFF_P, OFF_Q).reshape(seqs, dec, POOL_WIDTH)
    pos = pos0 + lax.broadcasted_iota(jnp.int32, (seqs, dec, POOL_GROUP_DIM), 1)

    def window_sum(g):
        lo = g * POOL_GROUP_DIM
        acc = pbuf[:, HIST_ROWS:HIST_ROWS + dec, lo:lo + POOL_GROUP_DIM]
        for i in range(1, POOL_WINDOWS[g]):
            acc = acc + pbuf[:, HIST_ROWS - i:HIST_ROWS - i + dec, lo:lo + POOL_GROUP_DIM]
        return acc

    def tok(g):
        lo = g * POOL_GROUP_DIM
        return pbuf[:, HIST_ROWS:HIST_ROWS + dec, lo:lo + POOL_GROUP_DIM]

    def cnt(g):
        return jnp.minimum(POOL_WINDOWS[g], pos + 1).astype(F32)

    br_b = _pool_branch(window_sum, tok, cnt, wpool_ref, ls_ref, wb_ref, rows)
    pool_ref[...] = pbuf[:, dec + 1:dec + HIST_ROWS, :]

    q = _project(xb, w_in_ref, b_in_ref, OFF_Q, OFF_G).astype(BF16)
    outs = []
    for h in range(MEM_HEADS):
        sl = slice(h * MEM_HEAD_DIM, (h + 1) * MEM_HEAD_DIM)
        qh = q[:, sl].reshape(seqs, dec, MEM_HEAD_DIM)
        kh = kc_ref[:, pl.ds(h, MEM_LEN, stride=MEM_HEADS), :].astype(BF16)
        vh = vc_ref[:, pl.ds(h, MEM_LEN, stride=MEM_HEADS), :].astype(BF16)
        s = jnp.einsum("sld,smd->slm", qh, kh,
                       preferred_element_type=F32) * (MEM_HEAD_DIM ** -0.5)
        o = jnp.einsum("slm,smd->sld", _softmax_rows(s).astype(BF16), vh,
                       preferred_element_type=F32)
        outs.append(o.reshape(rows, MEM_HEAD_DIM))
    br_m = _dot(jnp.concatenate(outs, axis=1).astype(BF16), wm_ref[...])

    gates = _branch_gates(xb, w_in_ref, b_in_ref)
    y_ref[...] = _merge_and_norm(x, gates, (br_a, br_b, br_m), wo_ref, ln1_g_ref, ln1_b_ref,
                                 alpha)


def _mix_sample(x2d, hist, kc, vc, wts, *, layer, nseq, dec, pos0, alpha):
    seqs = SAMPLE_SEQS
    rows = seqs * dec
    steps = nseq // seqs
    weight_specs = [_const_spec(w.shape) for w in wts]
    return pl.pallas_call(
        functools.partial(_mix_sample_kernel, alpha=alpha, pos0=pos0, dec=dec),
        grid=(steps,),
        in_specs=[pl.BlockSpec((rows, D_MODEL), lambda i: (i, 0)),
                  pl.BlockSpec((seqs, POOL_HIST, POOL_WIDTH), lambda i: (layer * steps + i, 0, 0)),
                  pl.BlockSpec((seqs, MEM_LEN * MEM_HEADS, MEM_HEAD_DIM),
                               lambda i: (layer * steps + i, 0, 0)),
                  pl.BlockSpec((seqs, MEM_LEN * MEM_HEADS, MEM_HEAD_DIM),
                               lambda i: (layer * steps + i, 0, 0))]
                 + weight_specs,
        out_specs=[pl.BlockSpec((rows, D_MODEL), lambda i: (i, 0)),
                   pl.BlockSpec((seqs, POOL_HIST, POOL_WIDTH), lambda i: (i, 0, 0)),
                   pl.BlockSpec((rows, A_WIDTH), lambda i: (i, 0))],
        out_shape=[jax.ShapeDtypeStruct((nseq * dec, D_MODEL), F32),
                   jax.ShapeDtypeStruct((nseq, POOL_HIST, POOL_WIDTH), F32),
                   jax.ShapeDtypeStruct((nseq * dec, A_WIDTH), F32)],
        scratch_shapes=[pltpu.VMEM((seqs, HIST_ROWS + dec, POOL_WIDTH), F32)],
        compiler_params=pltpu.CompilerParams(
            dimension_semantics=("arbitrary",),
            vmem_limit_bytes=V7X_VMEM_LIMIT_BYTES),
        name="mix_sample",
    )(x2d, hist, kc, vc, *wts)


def _split_bf16(a):
    hi = a.astype(BF16)
    lo = (a - hi.astype(F32)).astype(BF16)
    return hi, lo


def _router_kernel(yp_ref, ys_ref, wr_ref, br_ref, e_ref, g_ref, rank_ref, cnt_ref, carry, earlier,
                   *, prompt_tiles):
    i = pl.program_id(0)
    rows = yp_ref.shape[0]

    @pl.when(i == 0)
    def _():
        carry[...] = jnp.zeros(carry.shape, F32)
        before = (lax.broadcasted_iota(jnp.int32, (rows, rows), 0)
                  < lax.broadcasted_iota(jnp.int32, (rows, rows), 1))
        earlier[...] = jnp.where(before, 1.0, 0.0).astype(BF16)

    x = jnp.where(i < prompt_tiles, yp_ref[...], ys_ref[...])
    x_hi, x_lo = _split_bf16(x)
    w_hi, w_lo = _split_bf16(wr_ref[...])
    nt = (((1,), (1,)), ((), ()))
    logits = (lax.dot_general(w_hi, x_hi, nt, preferred_element_type=F32)
              + lax.dot_general(w_hi, x_lo, nt, preferred_element_type=F32)
              + lax.dot_general(w_lo, x_hi, nt, preferred_element_type=F32)) + br_ref[...]

    eid = lax.broadcasted_iota(jnp.int32, (N_EXPERTS, rows), 0)
    work = logits
    vals, idxs, sels = [], [], []
    for _ in range(TOP_K):
        m = jnp.max(work, axis=0, keepdims=True)
        idx = jnp.min(jnp.where(work == m, eid, N_EXPERTS), axis=0, keepdims=True)
        sel = eid == idx
        work = jnp.where(sel, -jnp.inf, work)
        vals.append(m)
        idxs.append(idx)
        sels.append(sel)
    top_v = jnp.concatenate(vals, axis=0)
    ex = jnp.exp(top_v - top_v[0:1])
    g_ref[...] = ex / jnp.sum(ex, axis=0, keepdims=True)
    e_ref[...] = jnp.concatenate(idxs, axis=0)

    onehot = jnp.zeros((N_EXPERTS, rows), F32)
    for sel in sels:
        onehot = onehot + jnp.where(sel, 1.0, 0.0)
    prefix = _dot(onehot.astype(BF16), earlier[...]) + carry[...]
    ranks = [jnp.sum(jnp.where(sel, prefix, 0.0), axis=0, keepdims=True) for sel in sels]
    rank_ref[...] = jnp.concatenate(ranks, axis=0).astype(jnp.int32)
    total = carry[...] + jnp.sum(onehot, axis=1, keepdims=True)
    carry[...] = total
    cnt_ref[...] = jnp.broadcast_to(total, cnt_ref.shape).astype(jnp.int32)


def _router(y_p, y_s, w_r_t, b_r_col):
    rows = ROUTER_ROWS
    tp, ts = y_p.shape[0], y_s.shape[0]
    pt, st = tp // rows, ts // rows
    total = tp + ts
    tok_spec = pl.BlockSpec((TOP_K, rows), lambda i: (0, i))
    return pl.pallas_call(
        functools.partial(_router_kernel, prompt_tiles=pt),
        grid=(pt + st,),
        in_specs=[pl.BlockSpec((rows, D_MODEL), lambda i: (jnp.minimum(i, pt - 1), 0)),
                  pl.BlockSpec((rows, D_MODEL), lambda i: (jnp.maximum(i - pt, 0), 0)),
                  _const_spec((N_EXPERTS, D_MODEL)),
                  _const_spec((N_EXPERTS, 1))],
        out_specs=[tok_spec, tok_spec, tok_spec,
                   pl.BlockSpec((N_EXPERTS, 128), lambda i: (0, 0))],
        out_shape=[jax.ShapeDtypeStruct((TOP_K, total), jnp.int32),
                   jax.ShapeDtypeStruct((TOP_K, total), F32),
                   jax.ShapeDtypeStruct((TOP_K, total), jnp.int32),
                   jax.ShapeDtypeStruct((N_EXPERTS, 128), jnp.int32)],
        scratch_shapes=[pltpu.VMEM((N_EXPERTS, 1), F32), pltpu.VMEM((rows, rows), BF16)],
        compiler_params=pltpu.CompilerParams(dimension_semantics=("arbitrary",)),
        name="router",
    )(y_p, y_s, w_r_t, b_r_col)


def _to_row_tiles(dst, val):
    rows = val.shape[0]
    for c in range(ROW_TILES):
        dst[pl.ds(c, rows, stride=ROW_TILES), :] = val[:, c * LANES:(c + 1) * LANES]


def _from_row_tiles(src, start, rows):
    return jnp.concatenate(
        [src[pl.ds(start + c, rows, stride=ROW_TILES), :] for c in range(ROW_TILES)], axis=1)


def _start_alternating(copy, k):
    copy.start(priority=k % 2)


def _dispatch_kernel(dest_ref, fill_lo_ref, fill_hi_ref, yp_ref, ys_ref, xs_ref, src, zbuf, sem,
                     *, prompt_tiles, total):
    i = pl.program_id(0)
    rows = yp_ref.shape[0]
    slot = lax.rem(i, 2)

    @pl.when(i == 0)
    def _():
        zbuf[...] = jnp.zeros(zbuf.shape, F32)

        def zero_block(b):
            at = pl.multiple_of(b * EXPERT_ROWS, EXPERT_ROWS)
            return pltpu.make_async_copy(zbuf, xs_ref.at[pl.ds(at, EXPERT_ROWS)], sem.at[2])

        def zero_padding(e, act):
            at, left = fill_lo_ref[e], fill_hi_ref[e] - fill_lo_ref[e]
            size = EXPERT_ROWS // 2
            while size:
                has = (left & size) != 0

                @pl.when(has)
                def _(at=at, size=size):
                    act(pltpu.make_async_copy(zbuf.at[pl.ds(0, size)],
                                              xs_ref.at[pl.ds(at, size)], sem.at[2]))

                at = at + jnp.where(has, size, 0)
                size //= 2

        def each_expert(act):
            lax.fori_loop(0, N_EXPERTS, lambda e, c: (zero_padding(e, act), c)[1], 0)

        lo, hi = fill_hi_ref[N_EXPERTS - 1] // EXPERT_ROWS, xs_ref.shape[0] // EXPERT_ROWS
        each_expert(lambda c: c.start())
        lax.fori_loop(lo, hi, lambda b, c: (zero_block(b).start(), c)[1], 0)
        each_expert(lambda c: c.wait())
        lax.fori_loop(lo, hi, lambda b, c: (zero_block(b).wait(), c)[1], 0)

    def row_copy(s, r, d):
        at = r * ROW_TILES
        if not isinstance(at, int):
            at = pl.multiple_of(at, ROW_TILES)
        return pltpu.make_async_copy(src.at[s, pl.ds(at, ROW_TILES)], xs_ref.at[d], sem.at[s])

    def copies(step, s, act):
        def body(r, carry):
            for k in range(TOP_K):
                act(row_copy(s, r, dest_ref[k * total + step * rows + r]), k)
            return carry
        lax.fori_loop(0, rows, body, 0, unroll=4)

    _to_row_tiles(src.at[slot], jnp.where(i < prompt_tiles, yp_ref[...], ys_ref[...]))
    for r in range(rows):
        for k in range(TOP_K):
            _start_alternating(row_copy(slot, r, dest_ref[k * total + i * rows + r]), k)

    @pl.when(i > 0)
    def _():
        for r in range(rows):
            for k in range(TOP_K):
                row_copy(1 - slot, r, dest_ref[k * total + (i - 1) * rows + r]).wait()

    @pl.when(i == pl.num_programs(0) - 1)
    def _():
        copies(i, slot, lambda c, k: c.wait())


def _dispatch(dest_flat, fill_lo, fill_hi, y_p, y_s, n_slots):
    rows = MOVE_ROWS
    tp, ts = y_p.shape[0], y_s.shape[0]
    pt, st = tp // rows, ts // rows
    return pl.pallas_call(
        functools.partial(_dispatch_kernel, prompt_tiles=pt, total=tp + ts),
        grid_spec=pltpu.PrefetchScalarGridSpec(
            num_scalar_prefetch=3,
            grid=(pt + st,),
            in_specs=[pl.BlockSpec((rows, D_MODEL), lambda i, *_: (jnp.minimum(i, pt - 1), 0)),
                      pl.BlockSpec((rows, D_MODEL), lambda i, *_: (jnp.maximum(i - pt, 0), 0))],
            out_specs=pl.BlockSpec(memory_space=pl.ANY),
            scratch_shapes=[pltpu.VMEM((2, rows * ROW_TILES, LANES), F32),
                            pltpu.VMEM((EXPERT_ROWS, ROW_TILES, LANES), F32),
                            pltpu.SemaphoreType.DMA((3,))]),
        out_shape=jax.ShapeDtypeStruct((n_slots, ROW_TILES, LANES), F32),
        compiler_params=pltpu.CompilerParams(dimension_semantics=("arbitrary",)),
        name="dispatch",
    )(dest_flat, fill_lo, fill_hi, y_p, y_s)


def _experts_kernel(blk_e_ref, first_ref, buf_ref, next_ref, nact_ref, x_ref, w1_hbm, b1_ref,
                    w2_hbm, b2_ref, o_ref, wf1, wf2, sems):
    b = pl.program_id(0)
    active = b < nact_ref[0]
    rows = x_ref.shape[0] // ROW_TILES

    def fetch(e, half):
        return (pltpu.make_async_copy(w1_hbm.at[e], wf1.at[half], sems.at[0, half]),
                pltpu.make_async_copy(w2_hbm.at[e], wf2.at[half], sems.at[1, half]))

    @pl.when(b == 0)
    def _():
        for c in fetch(blk_e_ref[0], buf_ref[0]):
            c.start()

    @pl.when(jnp.logical_not(active))
    def _():
        o_ref[...] = jnp.zeros(o_ref.shape, F32)

    @pl.when(active & (first_ref[b] == 1))
    def _():
        half = buf_ref[b]
        for c in fetch(blk_e_ref[b], half):
            c.wait()

        @pl.when(next_ref[b] >= 0)
        def _():
            for c in fetch(next_ref[b], 1 - half):
                c.start(priority=1)

    @pl.when(active)
    def _():
        half = buf_ref[b]
        x = _from_row_tiles(x_ref, 0, rows).astype(BF16)
        h = _dot(x, wf1[half].astype(BF16)) + b1_ref[...]
        g = jnp.minimum(h[:, :D_FF], SWIGLU_LIMIT)
        lin = jnp.clip(h[:, D_FF:], -SWIGLU_LIMIT, SWIGLU_LIMIT)
        act = g * jax.nn.sigmoid(SWIGLU_ALPHA * g) * (lin + 1.0)
        _to_row_tiles(o_ref, _dot(act.astype(BF16), wf2[half].astype(BF16)) + b2_ref[...])


def _experts(blk_e, blk_first, blk_buf, blk_next, nact, xs, w1, b1, w2, b2):
    rows = EXPERT_ROWS
    n_slots = xs.shape[0]

    def slot_map(b, blk_e, first, buf, nxt, nact):
        return (jnp.minimum(b, nact[0] - 1), 0)

    def expert_map(b, blk_e, first, buf, nxt, nact):
        return (blk_e[b], 0, 0)

    out = pl.pallas_call(
        _experts_kernel,
        grid_spec=pltpu.PrefetchScalarGridSpec(
            num_scalar_prefetch=5,
            grid=(n_slots // rows,),
            in_specs=[pl.BlockSpec((rows * ROW_TILES, LANES), slot_map),
                      pl.BlockSpec(memory_space=pl.ANY),
                      pl.BlockSpec((None, 1, 2 * D_FF), expert_map),
                      pl.BlockSpec(memory_space=pl.ANY),
                      pl.BlockSpec((None, 1, D_MODEL), expert_map)],
            out_specs=pl.BlockSpec((rows * ROW_TILES, LANES), lambda b, *_: (b, 0)),
            scratch_shapes=[pltpu.VMEM((2, D_MODEL, 2 * D_FF), F32),
                            pltpu.VMEM((2, D_FF, D_MODEL), F32),
                            pltpu.SemaphoreType.DMA((2, 2))]),
        out_shape=jax.ShapeDtypeStruct((n_slots * ROW_TILES, LANES), F32),
        compiler_params=pltpu.CompilerParams(
            dimension_semantics=("arbitrary",),
            vmem_limit_bytes=V7X_VMEM_LIMIT_BYTES),
        name="experts",
    )(blk_e, blk_first, blk_buf, blk_next, nact, xs.reshape(n_slots * ROW_TILES, LANES),
      w1, b1, w2, b2)
    return out.reshape(n_slots, ROW_TILES, LANES)


def _combine_kernel(dest_ref, y_ref, g_ref, os_ref, ln_g_ref, ln_b_ref, o_ref, buf_a, buf_b, sem,
                    *, first, total, alpha):
    i = pl.program_id(0)
    last = pl.num_programs(0) - 1
    rows = y_ref.shape[0]

    def row_copy(buf, s, k, r, d):
        at = (k * rows + r) * ROW_TILES
        if not isinstance(at, int):
            at = pl.multiple_of(at, ROW_TILES)
        return pltpu.make_async_copy(os_ref.at[d], buf.at[pl.ds(at, ROW_TILES)], sem.at[s])

    def slot_of(step, k, r):
        return dest_ref[k * total + first + step * rows + r]

    def looped(buf, s, step, act):
        def body(r, carry):
            for k in range(TOP_K):
                act(row_copy(buf, s, k, r, slot_of(step, k, r)), k)
            return carry
        lax.fori_loop(0, rows, body, 0, unroll=4)

    def start_unrolled(buf, s, step):
        for r in range(rows):
            for k in range(TOP_K):
                _start_alternating(row_copy(buf, s, k, r, slot_of(step, k, r)), k)

    def finish(buf):
        g = g_ref[...]
        f = g[:, 0:1] * _from_row_tiles(buf, 0, rows)
        for k in range(1, TOP_K):
            f = f + g[:, k:k + 1] * _from_row_tiles(buf, k * rows * ROW_TILES, rows)
        o_ref[...] = _layer_norm(alpha * y_ref[...] + f, ln_g_ref[...], ln_b_ref[...])

    @pl.when(i == 0)
    def _():
        looped(buf_a, 0, 0, _start_alternating)

    ahead = jnp.minimum(i + 1, last)
    for parity, (cur, nxt) in enumerate(((buf_a, buf_b), (buf_b, buf_a))):
        @pl.when(lax.rem(i, 2) == parity)
        def _():
            for r in range(rows):
                for k in range(TOP_K):
                    row_copy(cur, parity, k, r, slot_of(i, k, r)).wait()
            start_unrolled(nxt, 1 - parity, ahead)
            finish(cur)

            @pl.when(i == last)
            def _():
                looped(nxt, 1 - parity, ahead, lambda c, k: c.wait())


def _combine(dest_flat, y, gates, out_sorted, ln_g, ln_b, *, first, total, alpha):
    rows = MOVE_ROWS
    off = first // rows
    return pl.pallas_call(
        functools.partial(_combine_kernel, first=first, total=total, alpha=alpha),
        grid_spec=pltpu.PrefetchScalarGridSpec(
            num_scalar_prefetch=1,
            grid=(y.shape[0] // rows,),
            in_specs=[pl.BlockSpec((rows, D_MODEL), lambda i, d: (i, 0)),
                      pl.BlockSpec((rows, TOP_K), lambda i, d: (off + i, 0)),
                      pl.BlockSpec(memory_space=pl.ANY),
                      pl.BlockSpec((1, D_MODEL), lambda i, d: (0, 0)),
                      pl.BlockSpec((1, D_MODEL), lambda i, d: (0, 0))],
            out_specs=pl.BlockSpec((rows, D_MODEL), lambda i, d: (i, 0)),
            scratch_shapes=[pltpu.VMEM((TOP_K * rows * ROW_TILES, LANES), F32),
                            pltpu.VMEM((TOP_K * rows * ROW_TILES, LANES), F32),
                            pltpu.SemaphoreType.DMA((2,))]),
        out_shape=jax.ShapeDtypeStruct(y.shape, F32),
        compiler_params=pltpu.CompilerParams(dimension_semantics=("arbitrary",)),
        name="combine",
    )(dest_flat, y, gates, out_sorted, ln_g, ln_b)


def _moe(y_p, y_s, w_r, b_r, w1, b1, w2, b2, ln_g, ln_b, *, layer, alpha):
    total = y_p.shape[0] + y_s.shape[0]
    top_e, gates, rank, counts = _router(y_p, y_s, w_r.T, b_r.reshape(N_EXPERTS, 1))

    blk = EXPERT_ROWS
    n_blocks = -(-(total * TOP_K) // blk) + N_EXPERTS
    counts = counts[:, 0]
    padded = (counts + blk - 1) // blk * blk
    pad_end = jnp.cumsum(padded)
    pad_start = pad_end - padded
    experts = jnp.arange(N_EXPERTS, dtype=jnp.int32)[:, None, None]
    start_of = jnp.sum(jnp.where(top_e[None] == experts, pad_start[:, None, None], 0), axis=0)
    dest = (start_of + rank).reshape(TOP_K * total)
    nact = pad_end[-1] // blk
    blk_ids = jnp.minimum(jnp.arange(n_blocks, dtype=jnp.int32), nact - 1)
    blk_e = jnp.sum((pad_end[None, :] <= (blk_ids * blk)[:, None]).astype(jnp.int32), axis=1)
    blk_first = jnp.concatenate([jnp.ones((1,), jnp.int32),
                                 (blk_e[1:] != blk_e[:-1]).astype(jnp.int32)])
    blk_buf = (jnp.cumsum(blk_first) - 1) % 2
    ids = jnp.arange(N_EXPERTS, dtype=jnp.int32)
    later_used = (ids[None, :] > ids[:, None]) & (padded[None, :] > 0)
    next_e = jnp.min(jnp.where(later_used, ids[None, :], N_EXPERTS), axis=1)
    next_e = jnp.where(next_e < N_EXPERTS, next_e + layer * N_EXPERTS, -1)
    blk_next = jnp.sum(jnp.where(blk_e[:, None] == ids[None, :], next_e[None, :], 0), axis=1)

    xs = _dispatch(dest, pad_start + counts, pad_end, y_p, y_s, n_blocks * blk)
    out_sorted = _experts(blk_e + layer * N_EXPERTS, blk_first, blk_buf, blk_next, nact.reshape(1),
                          xs, w1.reshape(-1, D_MODEL, 2 * D_FF), b1.reshape(-1, 1, 2 * D_FF),
                          w2.reshape(-1, D_FF, D_MODEL), b2.reshape(-1, 1, D_MODEL))
    gates = gates.T
    ln_g, ln_b = ln_g.reshape(1, D_MODEL), ln_b.reshape(1, D_MODEL)
    return (_combine(dest, y_p, gates, out_sorted, ln_g, ln_b, first=0, total=total, alpha=alpha),
            _combine(dest, y_s, gates, out_sorted, ln_g, ln_b, first=y_p.shape[0], total=total,
                     alpha=alpha))


def kernel(x_prompt, x_sample, state_pool, cache_mem_k, cache_mem_v, mem_prompt, w_in, b_in, ln_v_g, ln_v_b, w_s, b_s, w_a_out, w_pool, ls_pool, w_b_out, w_mk, w_mv, w_m_out, w_o, ln1_g, ln1_b, w_r, b_r, w1, b1, w2, b2, ln2_g, ln2_b):
    depth = w_in.shape[0]
    batch, seq, _ = x_prompt.shape
    nseq, dec, _ = x_sample.shape
    pos0 = 16384
    alpha = (2 * depth) ** 0.25
    assert seq % PROMPT_ROWS == 0 and nseq % SAMPLE_SEQS == 0 and dec <= CHUNK and pos0 % CHUNK == 0
    assert (batch * seq) % ROUTER_ROWS == 0 and (nseq * dec) % ROUTER_ROWS == 0

    y_p = x_prompt.reshape(batch * seq, D_MODEL)
    y_s = x_sample.reshape(nseq * dec, D_MODEL)
    pool_p_out, pool_s_out, mk_out, mv_out, cv_out = [], [], [], [], []
    for l in range(depth):
        w_kv = jnp.concatenate([w_mk[l], w_mv[l]], axis=1).astype(BF16)
        bias = jnp.repeat(b_s[l].T, A_GROUP_DIM, axis=1)
        shared = (w_in[l].astype(BF16), b_in[l].reshape(1, C_IN), ln_v_g[l].reshape(1, A_WIDTH),
                  ln_v_b[l].reshape(1, A_WIDTH))
        tail = (w_a_out[l].astype(BF16), w_pool[l].astype(BF16), ls_pool[l].reshape(1, POOL_WIDTH),
                w_b_out[l].astype(BF16), w_m_out[l].astype(BF16), w_o[l].astype(BF16),
                ln1_g[l].reshape(1, D_MODEL), ln1_b[l].reshape(1, D_MODEL))
        y1_p, pool_p, mk_p, mv_p = _mix_prompt(
            y_p, mem_prompt.reshape(batch * MEM_LEN, D_MODEL), w_kv,
            shared + (w_s[l], bias) + tail, batch=batch, seq=seq, alpha=alpha)
        reps = SAMPLE_SEQS
        y1_s, pool_s, v_s = _mix_sample(
            y_s, state_pool.reshape(depth * nseq, POOL_HIST, POOL_WIDTH),
            cache_mem_k.reshape(depth * nseq, MEM_LEN * MEM_HEADS, MEM_HEAD_DIM),
            cache_mem_v.reshape(depth * nseq, MEM_LEN * MEM_HEADS, MEM_HEAD_DIM),
            shared + (jnp.tile(w_s[l][:, :dec, :dec], (1, reps, reps)),
                      jnp.tile(bias[:dec], (reps, 1))) + tail,
            layer=l, nseq=nseq, dec=dec, pos0=pos0, alpha=alpha)
        y_p, y_s = _moe(y1_p, y1_s, w_r[l], b_r[l], w1, b1, w2, b2, ln2_g[l], ln2_b[l],
                        layer=l, alpha=alpha)
        pool_p_out.append(pool_p)
        pool_s_out.append(pool_s)
        mk_out.append(mk_p.reshape(batch, MEM_LEN, MEM_HEADS, MEM_HEAD_DIM))
        mv_out.append(mv_p.reshape(batch, MEM_LEN, MEM_HEADS, MEM_HEAD_DIM))
        cv_out.append(v_s.reshape(nseq, dec, A_WIDTH))
    return (y_p.reshape(batch, seq, D_MODEL), y_s.reshape(nseq, dec, D_MODEL),
            jnp.stack(pool_p_out), jnp.stack(pool_s_out), jnp.stack(mk_out), jnp.stack(mv_out),
            jnp.stack(cv_out))
```
